```python
import math
import jax, jax.numpy as jnp
from jax import lax
import numpy as np

D_MODEL = 1024
BATCH = 8
SEQ = 2048
DEPTH = 2

CHUNK = 64
N_MIXERS = 2
N_HEADS = 16
HEAD_DIM = D_MODEL // N_HEADS
Q_BLOCK = 128
CONV_WIDTH = 31
D_FF = 2816
N_EXPERTS = 8
TOP_K = 2
D_FF_EXPERT = 1408
D_PLE = 256
LN_EPS = 1e-5
ALPHA = (2.0 * DEPTH) ** 0.25
BETA = (8.0 * DEPTH) ** -0.25
N_CONV = (DEPTH + 1) // 2
N_ATTN = DEPTH // 2

kernel_name = "hybrid_conv_stickbreaking_moe_deepnorm"


def layer_norm(x, g, b):
    xf = x.astype(jnp.float32)
    mu = jnp.mean(xf, axis=-1, keepdims=True)
    var = jnp.mean(jnp.square(xf - mu), axis=-1, keepdims=True)
    y = (xf - mu) * lax.rsqrt(var + LN_EPS)
    return (y * g + b).astype(x.dtype)


def conformer_conv(x, w_pw1, b_pw1, w_dw, b_dw, ln_g, ln_b, w_pw2, b_pw2):
    h = x @ w_pw1 + b_pw1
    a, g = jnp.split(h, 2, axis=-1)
    h = a * jax.nn.sigmoid(g)
    h = lax.conv_general_dilated(
        h, w_dw[:, None, :].astype(h.dtype),
        window_strides=(1,),
        padding=((CONV_WIDTH - 1, 0),),
        dimension_numbers=("NWC", "WIO", "NWC"),
        feature_group_count=D_MODEL) + b_dw
    h = jax.nn.silu(layer_norm(h, ln_g, ln_b))
    return h @ w_pw2 + b_pw2


def stick_breaking_attention(x, w_qkv, w_o):
    B, S, _ = x.shape
    qkv = (x @ w_qkv).reshape(B, S, 3, N_HEADS, HEAD_DIM)
    q, k, v = qkv[:, :, 0], qkv[:, :, 1], qkv[:, :, 2]
    scale = HEAD_DIM ** -0.5
    outs = []
    for qb in range(S // Q_BLOCK):
        q0 = qb * Q_BLOCK
        q1 = q0 + Q_BLOCK
        qi = q[:, q0:q1]
        kj = k[:, :q1]
        vj = v[:, :q1]
        z = jnp.einsum("bthd,bshd->bhts", qi, kj).astype(jnp.float32) * scale
        t_pos = q0 + jnp.arange(Q_BLOCK)[:, None]
        s_pos = jnp.arange(q1)[None, :]
        causal = s_pos < t_pos
        log_beta = jax.nn.log_sigmoid(z)
        log_1m_beta = jnp.where(causal, jax.nn.log_sigmoid(-z), 0.0)
        between = lax.cumsum(log_1m_beta, axis=log_1m_beta.ndim - 1, reverse=True) - log_1m_beta
        w = jnp.where(causal, jnp.exp(log_beta + between), 0.0)
        outs.append(jnp.einsum("bhts,bshd->bthd", w.astype(vj.dtype), vj))
    o = jnp.concatenate(outs, axis=1).reshape(B, S, D_MODEL)
    return o @ w_o


def swiglu(x, w_gate, w_up, w_down):
    return (jax.nn.silu(x @ w_gate) * (x @ w_up)) @ w_down


def moe_swiglu(x, w_router, b_router, w_gate, w_up, w_down):
    B, S, D = x.shape
    xt = x.reshape(B * S, D)
    logits = (xt @ w_router + b_router).astype(jnp.float32)
    top_val, top_idx = lax.top_k(logits, TOP_K)
    top_w = jax.nn.softmax(top_val, axis=-1)
    gates = jnp.sum(jax.nn.one_hot(top_idx, N_EXPERTS, dtype=jnp.float32)
                    * top_w[..., None], axis=1)
    y = jnp.zeros_like(xt)
    for e in range(N_EXPERTS):
        y = y + gates[:, e:e + 1].astype(xt.dtype) * swiglu(xt, w_gate[e], w_up[e], w_down[e])
    return y.reshape(B, S, D)


def setup_inputs(seed: int = 0) -> dict:
    key = jax.random.key(seed)
    ks = jax.random.split(key, 32)
    f32 = jnp.float32
    D = D_MODEL

    def nrm(k, shape, scale):
        return jax.random.normal(k, shape, f32) * scale

    return {
        "x": nrm(ks[0], (BATCH, SEQ, D), 1.0),
        "p": nrm(ks[1], (DEPTH, BATCH, SEQ, D_PLE), 1.0),
        "conv_w_pw1": nrm(ks[2], (N_CONV, D, 2 * D), D ** -0.5),
        "conv_b_pw1": nrm(ks[3], (N_CONV, 2 * D), 0.02),
        "conv_w_dw": nrm(ks[4], (N_CONV, CONV_WIDTH, D), CONV_WIDTH ** -0.5),
        "conv_b_dw": nrm(ks[5], (N_CONV, D), 0.02),
        "conv_ln_g": 1.0 + nrm(ks[6], (N_CONV, D), 0.02),
        "conv_ln_b": nrm(ks[7], (N_CONV, D), 0.02),
        "conv_w_pw2": nrm(ks[8], (N_CONV, D, D), BETA * D ** -0.5),
        "conv_b_pw2": nrm(ks[9], (N_CONV, D), 0.02),
        "attn_w_qkv": nrm(ks[10], (N_ATTN, D, 3 * D), D ** -0.5),
        "attn_w_o": nrm(ks[11], (N_ATTN, D, D), BETA * D ** -0.5),
        "ffn_w_gate": nrm(ks[12], (N_CONV, D, D_FF), D ** -0.5),
        "ffn_w_up": nrm(ks[13], (N_CONV, D, D_FF), D ** -0.5),
        "ffn_w_down": nrm(ks[14], (N_CONV, D_FF, D), BETA * D_FF ** -0.5),
        "moe_w_router": nrm(ks[15], (N_ATTN, D, N_EXPERTS), D ** -0.5),
        "moe_b_router": nrm(ks[16], (N_ATTN, N_EXPERTS), 0.01),
        "moe_w_gate": nrm(ks[17], (N_ATTN, N_EXPERTS, D, D_FF_EXPERT), D ** -0.5),
        "moe_w_up": nrm(ks[18], (N_ATTN, N_EXPERTS, D, D_FF_EXPERT), D ** -0.5),
        "moe_w_down": nrm(ks[19], (N_ATTN, N_EXPERTS, D_FF_EXPERT, D), BETA * D_FF_EXPERT ** -0.5),
        "ln_mix_g": 1.0 + nrm(ks[20], (DEPTH, D), 0.02),
        "ln_mix_b": nrm(ks[21], (DEPTH, D), 0.02),
        "ln_ffn_g": 1.0 + nrm(ks[22], (DEPTH, D), 0.02),
        "ln_ffn_b": nrm(ks[23], (DEPTH, D), 0.02),
        "ple_w_proj": nrm(ks[24], (DEPTH, D_PLE, D), D_PLE ** -0.5),
        "ple_w_gate": nrm(ks[25], (DEPTH, D, D), D ** -0.5),
        "ple_b_gate": nrm(ks[26], (DEPTH, D), 0.02),
    }


def reference(x, p, conv_w_pw1, conv_b_pw1, conv_w_dw, conv_b_dw, conv_ln_g, conv_ln_b,
              conv_w_pw2, conv_b_pw2, attn_w_qkv, attn_w_o, ffn_w_gate, ffn_w_up, ffn_w_down,
              moe_w_router, moe_b_router, moe_w_gate, moe_w_up, moe_w_down,
              ln_mix_g, ln_mix_b, ln_ffn_g, ln_ffn_b, ple_w_proj, ple_w_gate, ple_b_gate):
    h = x
    for i in range(DEPTH):
        j = i // N_MIXERS
        if i % N_MIXERS == 0:
            mix = conformer_conv(h, conv_w_pw1[j], conv_b_pw1[j], conv_w_dw[j], conv_b_dw[j],
                                 conv_ln_g[j], conv_ln_b[j], conv_w_pw2[j], conv_b_pw2[j])
        else:
            mix = stick_breaking_attention(h, attn_w_qkv[j], attn_w_o[j])
        h = layer_norm(ALPHA * h + mix, ln_mix_g[i], ln_mix_b[i])
        if i % 2 == 0:
            ff = swiglu(h, ffn_w_gate[j], ffn_w_up[j], ffn_w_down[j])
        else:
            ff = moe_swiglu(h, moe_w_router[j], moe_b_router[j], moe_w_gate[j],
                            moe_w_up[j], moe_w_down[j])
        h = layer_norm(ALPHA * h + ff, ln_ffn_g[i], ln_ffn_b[i])
        gate = jax.nn.sigmoid(h @ ple_w_gate[i] + ple_b_gate[i])
        h = h + gate * (p[i] @ ple_w_proj[i])
    return h
```

```python
import functools
import math

import jax
import jax.numpy as jnp
from jax import lax
from jax.experimental import pallas as pl
from jax.experimental.pallas import tpu as pltpu

F32 = jnp.float32
BF16 = jnp.bfloat16

D_MODEL = 1024
N_HEADS = 16
HEAD_DIM = D_MODEL // N_HEADS
CONV_WIDTH = 31
N_EXPERTS = 8
LN_EPS = 1e-5
DEPTH = 2
ALPHA = (2.0 * DEPTH) ** 0.25

LANES = 128
SUBLANES = 8
VMEM_LIMIT = 56 * 1024 * 1024

CONV_TM = 256
CONV_HALO = 32
CONV_RC = 32
FFN_TM = 512
QKV_TM = 512
ATT_T = 256
OUT_TM = 512
MOE_TM = 512


def _ln(x, g, b):
    mu = jnp.mean(x, axis=-1, keepdims=True)
    xc = x - mu
    var = jnp.mean(xc * xc, axis=-1, keepdims=True)
    return xc * lax.rsqrt(var + LN_EPS) * g + b


def _sigmoid(x):
    return 1.0 / (1.0 + jnp.exp(-x))


def _dot(a, b):
    return jnp.dot(a, b, preferred_element_type=F32)


def _const_spec(shape):
    nd = len(shape)
    return pl.BlockSpec(shape, lambda *_: (0,) * nd, pipeline_mode=pl.Buffered(1))


def _params(n_axes):
    return pltpu.CompilerParams(dimension_semantics=("arbitrary",) * n_axes,
                                vmem_limit_bytes=VMEM_LIMIT)


def _conv_mixer_kernel(x_ref, w1_ref, b1_ref, wdw_ref, bdw_ref, lng_ref, lnb_ref, w2_ref, b2_ref,
                       mg_ref, mb_ref, o_ref, g_scr, c_scr):
    tm = x_ref.shape[0]
    d = x_ref.shape[1]

    @pl.when(pl.program_id(1) == 0)
    def _():
        g_scr[0:CONV_HALO, :] = jnp.zeros((CONV_HALO, d), F32)

    x = x_ref[...]
    h = _dot(x.astype(BF16), w1_ref[...]) + b1_ref[...]
    g_scr[CONV_HALO:CONV_HALO + tm, :] = h[:, :d] * _sigmoid(h[:, d:])

    base = CONV_HALO - (CONV_WIDTH - 1)
    for c in range(tm // CONV_RC):
        acc = jnp.broadcast_to(bdw_ref[...], (CONV_RC, d))
        for k in range(CONV_WIDTH):
            acc = acc + wdw_ref[k:k + 1, :] * g_scr[pl.ds(c * CONV_RC + base + k, CONV_RC), :]
        c_scr[c * CONV_RC:(c + 1) * CONV_RC, :] = acc

    g_scr[0:CONV_HALO, :] = g_scr[tm:tm + CONV_HALO, :]

    hc = _ln(c_scr[...], lng_ref[...], lnb_ref[...])
    hc = hc * _sigmoid(hc)
    mix = _dot(hc.astype(BF16), w2_ref[...]) + b2_ref[...]
    o_ref[...] = _ln(ALPHA * x + mix, mg_ref[...], mb_ref[...])


def _conv_mixer(x2d, batch, seq, w1, b1, wdw, bdw, lng, lnb, w2, b2, mg, mb):
    t, d = x2d.shape
    tm = CONV_TM
    ns = seq // tm
    row = lambda v: v.reshape(1, -1)
    return pl.pallas_call(
        _conv_mixer_kernel,
        grid=(batch, ns),
        in_specs=[
            pl.BlockSpec((tm, d), lambda b, s: (b * ns + s, 0)),
            _const_spec((d, 2 * d)), _const_spec((1, 2 * d)),
            _const_spec((CONV_WIDTH, d)), _const_spec((1, d)),
            _const_spec((1, d)), _const_spec((1, d)),
            _const_spec((d, d)), _const_spec((1, d)),
            _const_spec((1, d)), _const_spec((1, d)),
        ],
        out_specs=pl.BlockSpec((tm, d), lambda b, s: (b * ns + s, 0)),
        out_shape=jax.ShapeDtypeStruct((t, d), F32),
        scratch_shapes=[pltpu.VMEM((CONV_HALO + tm, d), F32), pltpu.VMEM((tm, d), F32)],
        compiler_params=_params(2),
        name="conv_mixer",
    )(x2d, w1, row(b1), wdw, row(bdw), row(lng), row(lnb), w2, row(b2), row(mg), row(mb))


def _ffn_epilogue(h, ff, lg_ref, lb_ref, p_ref, wproj_ref, wpg_ref, bpg_ref):
    y = _ln(ALPHA * h + ff, lg_ref[...], lb_ref[...])
    gate = _sigmoid(_dot(y.astype(BF16), wpg_ref[...]) + bpg_ref[...])
    return y + gate * _dot(p_ref[...].astype(BF16), wproj_ref[...])


def _ffn0_kernel(h_ref, wg_ref, wu_ref, wd_ref, lg_ref, lb_ref, p_ref, wproj_ref, wpg_ref, bpg_ref, o_ref):
    h = h_ref[...]
    hb = h.astype(BF16)
    gate = _dot(hb, wg_ref[...])
    up = _dot(hb, wu_ref[...])
    act = (gate * _sigmoid(gate) * up).astype(BF16)
    ff = _dot(act, wd_ref[...])
    o_ref[...] = _ffn_epilogue(h, ff, lg_ref, lb_ref, p_ref, wproj_ref, wpg_ref, bpg_ref)


def _ffn0(h, wg, wu, wd, lg, lb, p2d, wproj, wpg, bpg):
    t, d = h.shape
    f = wg.shape[1]
    dp = p2d.shape[1]
    tm = FFN_TM
    row = lambda v: v.reshape(1, -1)
    return pl.pallas_call(
        _ffn0_kernel,
        grid=(t // tm,),
        in_specs=[
            pl.BlockSpec((tm, d), lambda i: (i, 0)),
            _const_spec((d, f)), _const_spec((d, f)), _const_spec((f, d)),
            _const_spec((1, d)), _const_spec((1, d)),
            pl.BlockSpec((tm, dp), lambda i: (i, 0)),
            _const_spec((dp, d)), _const_spec((d, d)), _const_spec((1, d)),
        ],
        out_specs=pl.BlockSpec((tm, d), lambda i: (i, 0)),
        out_shape=jax.ShapeDtypeStruct((t, d), F32),
        compiler_params=_params(1),
        name="ffn0",
    )(h, wg, wu, wd, row(lg), row(lb), p2d, wproj, wpg, row(bpg))


def _qkv_kernel(h_ref, w_ref, o_ref):
    d = h_ref.shape[1]
    qkv = _dot(h_ref[...].astype(BF16), w_ref[...])
    o_ref[:, :d] = (qkv[:, :d] * (HEAD_DIM ** -0.5)).astype(BF16)
    o_ref[:, d:] = qkv[:, d:].astype(BF16)


def _qkv(h, w):
    t, d = h.shape
    n = w.shape[1]
    tm = QKV_TM
    return pl.pallas_call(
        _qkv_kernel,
        grid=(t // tm,),
        in_specs=[pl.BlockSpec((tm, d), lambda i: (i, 0)), _const_spec((d, n))],
        out_specs=pl.BlockSpec((tm, n), lambda i: (i, 0)),
        out_shape=jax.ShapeDtypeStruct((t, n), BF16),
        compiler_params=_params(1),
        name="qkv_proj",
    )(h, w)


def _attn_kernel(q_ref, k_ref, v_ref, tri_ref, o_ref, qs_scr, acc_scr, r_scr):
    tq = q_ref.shape[0]
    qi = pl.program_id(2)

    q = q_ref[...]
    lane = lax.broadcasted_iota(jnp.int32, q.shape, 1)
    zero = jnp.zeros_like(q)
    qs_scr[0:tq, :] = jnp.where(lane < HEAD_DIM, q, zero)
    qs_scr[tq:2 * tq, :] = jnp.where(lane >= HEAD_DIM, q, zero)
    acc_scr[...] = jnp.zeros(acc_scr.shape, F32)
    r_scr[...] = jnp.zeros(r_scr.shape, F32)

    def block(kb, masked):
        start = pl.multiple_of(kb * tq, tq)
        kblk = k_ref[pl.ds(start, tq), :]
        vblk = v_ref[pl.ds(start, tq), :]
        z = lax.dot_general(qs_scr[...], kblk, (((1,), (1,)), ((), ())), preferred_element_type=F32)
        log1m = -(jnp.maximum(z, 0.0) + jnp.log(1.0 + jnp.exp(-jnp.abs(z))))
        if masked:
            rows = lax.broadcasted_iota(jnp.int32, z.shape, 0) & (tq - 1)
            cols = lax.broadcasted_iota(jnp.int32, z.shape, 1)
            causal = cols < rows
            log1m = jnp.where(causal, log1m, 0.0)
        hi = log1m.astype(BF16)
        lo = (log1m - hi.astype(F32)).astype(BF16)
        csum = _dot(hi, tri_ref[...]) + _dot(lo, tri_ref[...])
        w = jnp.exp(z + csum + r_scr[...])
        if masked:
            w = jnp.where(causal, w, 0.0)
        r_scr[...] = r_scr[...] + csum[:, 0:1]
        acc_scr[...] = acc_scr[...] + _dot(w.astype(BF16), vblk)

    block(qi, True)

    def body(it, carry):
        block(qi - 1 - it, False)
        return carry

    lax.fori_loop(0, qi, body, 0)

    lane_o = lax.broadcasted_iota(jnp.int32, (tq, LANES), 1)
    o_ref[...] = jnp.where(lane_o < HEAD_DIM, acc_scr[0:tq, :], acc_scr[tq:2 * tq, :]).astype(BF16)


def _attention(qkv, batch, seq):
    t = qkv.shape[0]
    tq = ATT_T
    nq = seq // tq
    hp = D_MODEL // LANES
    j = lax.broadcasted_iota(jnp.int32, (tq, tq), 0)
    s = lax.broadcasted_iota(jnp.int32, (tq, tq), 1)
    tri = (j >= s).astype(BF16)
    return pl.pallas_call(
        _attn_kernel,
        grid=(batch, hp, nq),
        in_specs=[
            pl.BlockSpec((tq, LANES), lambda b, h, i: (b * nq + i, h)),
            pl.BlockSpec((seq, LANES), lambda b, h, i: (b, hp + h)),
            pl.BlockSpec((seq, LANES), lambda b, h, i: (b, 2 * hp + h)),
            _const_spec((tq, tq)),
        ],
        out_specs=pl.BlockSpec((tq, LANES), lambda b, h, i: (b * nq + i, h)),
        out_shape=jax.ShapeDtypeStruct((t, D_MODEL), BF16),
        scratch_shapes=[pltpu.VMEM((2 * tq, LANES), BF16), pltpu.VMEM((2 * tq, LANES), F32),
                        pltpu.VMEM((2 * tq, 1), F32)],
        compiler_params=_params(3),
        name="stick_attn",
    )(qkv, qkv, qkv, tri)


def _attn_out_kernel(o_ref, h_ref, wo_ref, mg_ref, mb_ref, wrh_ref, wrl_ref, br_ref, y_ref, gates_ref):
    y = _ln(ALPHA * h_ref[...] + _dot(o_ref[...], wo_ref[...]), mg_ref[...], mb_ref[...])
    y_ref[...] = y
    yh = y.astype(BF16)
    yl = (y - yh.astype(F32)).astype(BF16)
    logits = _dot(yh, wrh_ref[...]) + _dot(yl, wrh_ref[...]) + _dot(yh, wrl_ref[...]) + br_ref[...]
    lane = lax.broadcasted_iota(jnp.int32, logits.shape, 1)
    m1 = jnp.max(logits, axis=-1, keepdims=True)
    i1 = jnp.min(jnp.where(logits == m1, lane, LANES), axis=-1, keepdims=True)
    rest = jnp.where(lane == i1, -jnp.inf, logits)
    m2 = jnp.max(rest, axis=-1, keepdims=True)
    i2 = jnp.min(jnp.where(rest == m2, lane, LANES), axis=-1, keepdims=True)
    e = jnp.exp(m2 - m1)
    w1 = 1.0 / (1.0 + e)
    w2 = e / (1.0 + e)
    gates_ref[...] = jnp.where(lane == i1, w1, 0.0) + jnp.where(lane == i2, w2, 0.0)


def _attn_out(o, h, wo, mg, mb, wr_hi, wr_lo, br):
    t, d = h.shape
    tm = OUT_TM
    row = lambda v: v.reshape(1, -1)
    return pl.pallas_call(
        _attn_out_kernel,
        grid=(t // tm,),
        in_specs=[
            pl.BlockSpec((tm, d), lambda i: (i, 0)),
            pl.BlockSpec((tm, d), lambda i: (i, 0)),
            _const_spec((d, d)), _const_spec((1, d)), _const_spec((1, d)),
            _const_spec((d, LANES)), _const_spec((d, LANES)), _const_spec((1, LANES)),
        ],
        out_specs=[pl.BlockSpec((tm, d), lambda i: (i, 0)), pl.BlockSpec((tm, LANES), lambda i: (i, 0))],
        out_shape=[jax.ShapeDtypeStruct((t, d), F32), jax.ShapeDtypeStruct((t, LANES), F32)],
        compiler_params=_params(1),
        name="attn_out_router",
    )(o, h, wo, row(mg), row(mb), wr_hi, wr_lo, br)


def _moe_kernel(h_ref, gates_ref, wg_ref, wu_ref, wd_ref, lg_ref, lb_ref, p_ref, wproj_ref, wpg_ref, bpg_ref,
                o_ref, acc_scr, hb_scr):
    e = pl.program_id(1)

    @pl.when(e == 0)
    def _():
        acc_scr[...] = jnp.zeros(acc_scr.shape, F32)
        hb_scr[...] = h_ref[...].astype(BF16)

    hb = hb_scr[...]
    gate = _dot(hb, wg_ref[...])
    up = _dot(hb, wu_ref[...])
    act = (gate * _sigmoid(gate) * up).astype(BF16)
    y = _dot(act, wd_ref[...])
    gates = gates_ref[...]
    lane = lax.broadcasted_iota(jnp.int32, gates.shape, 1)
    gcol = jnp.sum(jnp.where(lane == e, gates, 0.0), axis=-1, keepdims=True)
    acc_scr[...] = acc_scr[...] + gcol * y

    @pl.when(e == pl.num_programs(1) - 1)
    def _():
        o_ref[...] = _ffn_epilogue(h_ref[...], acc_scr[...], lg_ref, lb_ref, p_ref, wproj_ref, wpg_ref, bpg_ref)


def _moe(h, gates, wg, wu, wd, lg, lb, p2d, wproj, wpg, bpg):
    t, d = h.shape
    ne, _, f = wg.shape
    dp = p2d.shape[1]
    tm = MOE_TM
    row = lambda v: v.reshape(1, -1)
    return pl.pallas_call(
        _moe_kernel,
        grid=(t // tm, ne),
        in_specs=[
            pl.BlockSpec((tm, d), lambda i, e: (i, 0)),
            pl.BlockSpec((tm, LANES), lambda i, e: (i, 0)),
            pl.BlockSpec((None, d, f), lambda i, e: (e, 0, 0)),
            pl.BlockSpec((None, d, f), lambda i, e: (e, 0, 0)),
            pl.BlockSpec((None, f, d), lambda i, e: (e, 0, 0)),
            _const_spec((1, d)), _const_spec((1, d)),
            pl.BlockSpec((tm, dp), lambda i, e: (i, 0)),
            _const_spec((dp, d)), _const_spec((d, d)), _const_spec((1, d)),
        ],
        out_specs=pl.BlockSpec((tm, d), lambda i, e: (i, 0)),
        out_shape=jax.ShapeDtypeStruct((t, d), F32),
        scratch_shapes=[pltpu.VMEM((tm, d), F32), pltpu.VMEM((tm, d), BF16)],
        compiler_params=_params(2),
        name="moe_experts",
    )(h, gates, wg, wu, wd, row(lg), row(lb), p2d, wproj, wpg, row(bpg))


def kernel(x, p, conv_w_pw1, conv_b_pw1, conv_w_dw, conv_b_dw, conv_ln_g, conv_ln_b, conv_w_pw2, conv_b_pw2, attn_w_qkv, attn_w_o, ffn_w_gate, ffn_w_up, ffn_w_down, moe_w_router, moe_b_router, moe_w_gate, moe_w_up, moe_w_down, ln_mix_g, ln_mix_b, ln_ffn_g, ln_ffn_b, ple_w_proj, ple_w_gate, ple_b_gate):
    batch, seq, d = x.shape
    t = batch * seq
    bf = lambda w: w.astype(BF16)
    x2d = x.reshape(t, d)
    p2d = p.reshape(p.shape[0], t, p.shape[-1])

    h = _conv_mixer(x2d, batch, seq, bf(conv_w_pw1[0]), conv_b_pw1[0], conv_w_dw[0], conv_b_dw[0],
                    conv_ln_g[0], conv_ln_b[0], bf(conv_w_pw2[0]), conv_b_pw2[0], ln_mix_g[0], ln_mix_b[0])
    h = _ffn0(h, bf(ffn_w_gate[0]), bf(ffn_w_up[0]), bf(ffn_w_down[0]), ln_ffn_g[0], ln_ffn_b[0],
              p2d[0], bf(ple_w_proj[0]), bf(ple_w_gate[0]), ple_b_gate[0])

    qkv = _qkv(h, bf(attn_w_qkv[0]))
    o = _attention(qkv, batch, seq)
    wr = jnp.zeros((d, LANES), F32).at[:, :N_EXPERTS].set(moe_w_router[0])
    wr_hi = wr.astype(BF16)
    wr_lo = (wr - wr_hi.astype(F32)).astype(BF16)
    br = jnp.full((1, LANES), -1e30, F32).at[0, :N_EXPERTS].set(moe_b_router[0])
    h, gates = _attn_out(o, h, bf(attn_w_o[0]), ln_mix_g[1], ln_mix_b[1], wr_hi, wr_lo, br)
    h = _moe(h, gates, bf(moe_w_gate[0]), bf(moe_w_up[0]), bf(moe_w_down[0]), ln_ffn_g[1], ln_ffn_b[1],
             p2d[1], bf(ple_w_proj[1]), bf(ple_w_gate[1]), ple_b_gate[1])
    return h.reshape(batch, seq, d)
```

```python
import functools
import math

import jax
import jax.numpy as jnp
from jax import lax
from jax.experimental import pallas as pl
from jax.experimental.pallas import tpu as pltpu

F32 = jnp.float32
BF16 = jnp.bfloat16

D_MODEL = 1024
N_HEADS = 16
HEAD_DIM = D_MODEL // N_HEADS
CONV_WIDTH = 31
N_EXPERTS = 8
LN_EPS = 1e-5
DEPTH = 2
ALPHA = (2.0 * DEPTH) ** 0.25
LOG2E = math.log2(math.e)

LANES = 128
SUBLANES = 8
VMEM_LIMIT = 56 * 1024 * 1024

CONV_TM = 512
CONV_HALO = 32
FFN_TM = 512
QKV_TM = 512
ATT_T = 256
OUT_TM = 512
MOE_TM = 512


def _ln(x, g, b):
    mu = jnp.mean(x, axis=-1, keepdims=True)
    xc = x - mu
    var = jnp.mean(xc * xc, axis=-1, keepdims=True)
    return xc * lax.rsqrt(var + LN_EPS) * g + b


def _sigmoid(x):
    return 1.0 / (1.0 + jnp.exp(-x))


def _dot(a, b):
    return jnp.dot(a, b, preferred_element_type=F32)


def _const_spec(shape):
    nd = len(shape)
    return pl.BlockSpec(shape, lambda *_: (0,) * nd, pipeline_mode=pl.Buffered(1))


def _params(n_axes):
    return pltpu.CompilerParams(dimension_semantics=("arbitrary",) * n_axes,
                                vmem_limit_bytes=VMEM_LIMIT)


def _dwconv(g_scr, wb_ref, bdw_ref, c_scr, tm):
    d = g_scr.shape[1]
    off = CONV_HALO - (CONV_WIDTH - 1)
    sub = lax.broadcasted_iota(jnp.int32, (SUBLANES, LANES), 0)
    masks = [sub >= r for r in range(SUBLANES)]
    n_a = (off + CONV_WIDTH - 1) // SUBLANES

    def valid(m):
        return off <= m < off + CONV_WIDTH

    for l in range(d // LANES):
        cols = slice(l * LANES, (l + 1) * LANES)

        def wv(m, cols=cols):
            return wb_ref[m - off, :, cols]

        def tile(j, cols=cols):
            if isinstance(j, int):
                return g_scr[j * SUBLANES:(j + 1) * SUBLANES, cols]
            return g_scr[pl.ds(pl.multiple_of(j * SUBLANES, SUBLANES), SUBLANES), cols]

        def shifted_sums(t, wv=wv):
            out = []
            for r in range(1, SUBLANES):
                acc = None
                for a in range(n_a):
                    m = SUBLANES * a + r
                    if valid(m):
                        term = wv(m) * t[a]
                        acc = term if acc is None else acc + term
                out.append(acc)
            return tuple(out)

        bias = jnp.broadcast_to(bdw_ref[:, cols], (SUBLANES, LANES))

        def body(i, carry, cols=cols, bias=bias, wv=wv, tile=tile, shifted_sums=shifted_sums):
            t = [tile(i + 1 + a) for a in range(n_a)]
            nxt = shifted_sums(t)
            acc = bias
            for a in range(1, n_a + 1):
                if valid(SUBLANES * a):
                    acc = acc + wv(SUBLANES * a) * t[a - 1]
            for r in range(1, SUBLANES):
                y = jnp.where(masks[r], carry[r - 1], nxt[r - 1])
                acc = acc + pltpu.roll(y, SUBLANES - r, 0)
            c_scr[pl.ds(pl.multiple_of(i * SUBLANES, SUBLANES), SUBLANES), cols] = acc
            return nxt

        lax.fori_loop(0, tm // SUBLANES, body, shifted_sums([tile(a) for a in range(n_a)]), unroll=4)


def _conv_mixer_kernel(x_ref, w1_ref, b1_ref, wb_ref, bdw_ref, lng_ref, lnb_ref, w2_ref, b2_ref,
                       mg_ref, mb_ref, o_ref, g_scr, c_scr):
    tm = x_ref.shape[0]
    d = x_ref.shape[1]

    @pl.when(pl.program_id(1) == 0)
    def _():
        g_scr[0:CONV_HALO, :] = jnp.zeros((CONV_HALO, d), F32)

    x = x_ref[...]
    h = _dot(x.astype(BF16), w1_ref[...]) + b1_ref[...]
    g_scr[CONV_HALO:CONV_HALO + tm, :] = h[:, :d] * _sigmoid(h[:, d:])
    _dwconv(g_scr, wb_ref, bdw_ref, c_scr, tm)
    g_scr[0:CONV_HALO, :] = g_scr[tm:tm + CONV_HALO, :]

    hc = _ln(c_scr[...], lng_ref[...], lnb_ref[...])
    hc = hc * _sigmoid(hc)
    mix = _dot(hc.astype(BF16), w2_ref[...]) + b2_ref[...]
    o_ref[...] = _ln(ALPHA * x + mix, mg_ref[...], mb_ref[...])


def _conv_mixer(x2d, batch, seq, w1, b1, wdw, bdw, lng, lnb, w2, b2, mg, mb):
    t, d = x2d.shape
    tm = CONV_TM
    ns = seq // tm
    row = lambda v: v.reshape(1, -1)
    return pl.pallas_call(
        _conv_mixer_kernel,
        grid=(batch, ns),
        in_specs=[
            pl.BlockSpec((tm, d), lambda b, s: (b * ns + s, 0)),
            _const_spec((d, 2 * d)), _const_spec((1, 2 * d)),
            _const_spec((CONV_WIDTH, SUBLANES, d)), _const_spec((1, d)),
            _const_spec((1, d)), _const_spec((1, d)),
            _const_spec((d, d)), _const_spec((1, d)),
            _const_spec((1, d)), _const_spec((1, d)),
        ],
        out_specs=pl.BlockSpec((tm, d), lambda b, s: (b * ns + s, 0)),
        out_shape=jax.ShapeDtypeStruct((t, d), F32),
        scratch_shapes=[pltpu.VMEM((CONV_HALO + tm, d), F32), pltpu.VMEM((tm, d), F32)],
        compiler_params=_params(2),
        name="conv_mixer",
    )(x2d, w1, row(b1), jnp.broadcast_to(wdw[:, None, :], (CONV_WIDTH, SUBLANES, d)), row(bdw),
      row(lng), row(lnb), w2, row(b2), row(mg), row(mb))


def _ffn_epilogue(h, ff, lg_ref, lb_ref, p_ref, wproj_ref, wpg_ref, bpg_ref):
    y = _ln(ALPHA * h + ff, lg_ref[...], lb_ref[...])
    gate = _sigmoid(_dot(y.astype(BF16), wpg_ref[...]) + bpg_ref[...])
    return y + gate * _dot(p_ref[...].astype(BF16), wproj_ref[...])


def _ffn0_kernel(h_ref, wg_ref, wu_ref, wd_ref, lg_ref, lb_ref, p_ref, wproj_ref, wpg_ref, bpg_ref, o_ref):
    h = h_ref[...]
    hb = h.astype(BF16)
    gate = _dot(hb, wg_ref[...])
    up = _dot(hb, wu_ref[...])
    act = (gate * _sigmoid(gate) * up).astype(BF16)
    ff = _dot(act, wd_ref[...])
    o_ref[...] = _ffn_epilogue(h, ff, lg_ref, lb_ref, p_ref, wproj_ref, wpg_ref, bpg_ref)


def _ffn0(h, wg, wu, wd, lg, lb, p3d, layer, wproj, wpg, bpg):
    t, d = h.shape
    f = wg.shape[1]
    dp = p3d.shape[2]
    tm = FFN_TM
    row = lambda v: v.reshape(1, -1)
    return pl.pallas_call(
        _ffn0_kernel,
        grid=(t // tm,),
        in_specs=[
            pl.BlockSpec((tm, d), lambda i: (i, 0)),
            _const_spec((d, f)), _const_spec((d, f)), _const_spec((f, d)),
            _const_spec((1, d)), _const_spec((1, d)),
            pl.BlockSpec((None, tm, dp), lambda i: (layer, i, 0)),
            _const_spec((dp, d)), _const_spec((d, d)), _const_spec((1, d)),
        ],
        out_specs=pl.BlockSpec((tm, d), lambda i: (i, 0)),
        out_shape=jax.ShapeDtypeStruct((t, d), F32),
        compiler_params=_params(1),
        name="ffn0",
    )(h, wg, wu, wd, row(lg), row(lb), p3d, wproj, wpg, row(bpg))


def _qkv_kernel(h_ref, w_ref, o_ref):
    d = h_ref.shape[1]
    qkv = _dot(h_ref[...].astype(BF16), w_ref[...])
    o_ref[:, :d] = (qkv[:, :d] * (HEAD_DIM ** -0.5 * LOG2E)).astype(BF16)
    o_ref[:, d:] = qkv[:, d:].astype(BF16)


def _qkv(h, w):
    t, d = h.shape
    n = w.shape[1]
    tm = QKV_TM
    return pl.pallas_call(
        _qkv_kernel,
        grid=(t // tm,),
        in_specs=[pl.BlockSpec((tm, d), lambda i: (i, 0)), _const_spec((d, n))],
        out_specs=pl.BlockSpec((tm, n), lambda i: (i, 0)),
        out_shape=jax.ShapeDtypeStruct((t, n), BF16),
        compiler_params=_params(1),
        name="qkv_proj",
    )(h, w)


def _attn_kernel(q_ref, k_ref, v_ref, tri_ref, o_ref, qs_scr, acc_scr, r_scr, *stage_scr):
    tq = q_ref.shape[0]
    qi = pl.program_id(2)
    stage_a, stage_b = stage_scr[:3], stage_scr[3:]

    nq = -q_ref[...]
    lane = lax.broadcasted_iota(jnp.int32, nq.shape, 1)
    zero = jnp.zeros_like(nq)
    qs_scr[0:tq, :] = jnp.where(lane < HEAD_DIM, nq, zero)
    qs_scr[tq:2 * tq, :] = jnp.where(lane >= HEAD_DIM, nq, zero)
    acc_scr[...] = jnp.zeros(acc_scr.shape, F32)
    r_scr[...] = jnp.zeros(r_scr.shape, F32)

    def rows(h):
        return slice(h * tq, (h + 1) * tq)

    def score_matmul(kb, h):
        start = pl.multiple_of(kb * tq, tq)
        return lax.dot_general(qs_scr[rows(h), :], k_ref[pl.ds(start, tq), :], (((1,), (1,)), ((), ())),
                               preferred_element_type=F32)

    def score_rest(zn, stage, h, masked=False):
        zn_scr, hi_scr, lo_scr = stage
        if masked:
            row = lax.broadcasted_iota(jnp.int32, zn.shape, 0)
            col = lax.broadcasted_iota(jnp.int32, zn.shape, 1)
            zn = jnp.where(col < row, zn, jnp.inf)
        log1m = jnp.minimum(zn, 0.0) - jnp.log(1.0 + jnp.exp2(-jnp.abs(zn))) * LOG2E
        hi = log1m.astype(BF16)
        zn_scr[rows(h), :] = zn
        hi_scr[rows(h), :] = hi
        lo_scr[rows(h), :] = (log1m - hi.astype(F32)).astype(BF16)

    def cumsum_matmul(stage, h):
        _, hi_scr, lo_scr = stage
        return _dot(hi_scr[rows(h), :], tri_ref[...]) + _dot(lo_scr[rows(h), :], tri_ref[...])

    def accumulate_rest(csum, kb, stage, h):
        start = pl.multiple_of(kb * tq, tq)
        r = r_scr[rows(h), :]
        w = jnp.exp2(csum + r - stage[0][rows(h), :])
        r_scr[rows(h), :] = r + csum[:, 0:1]
        acc_scr[rows(h), :] = acc_scr[rows(h), :] + _dot(w.astype(BF16), v_ref[pl.ds(start, tq), :])

    score_rest(score_matmul(qi, 0), stage_a, 0, masked=True)
    score_rest(score_matmul(qi, 1), stage_a, 1, masked=True)

    def pair(p, carry):
        kb = qi - 2 * p
        c0 = cumsum_matmul(stage_a, 0)
        z0 = score_matmul(kb - 1, 0)
        c1 = cumsum_matmul(stage_a, 1)
        z1 = score_matmul(kb - 1, 1)
        accumulate_rest(c0, kb, stage_a, 0)
        score_rest(z0, stage_b, 0)
        c2 = cumsum_matmul(stage_b, 0)
        z2 = score_matmul(kb - 2, 0)
        accumulate_rest(c1, kb, stage_a, 1)
        score_rest(z1, stage_b, 1)
        c3 = cumsum_matmul(stage_b, 1)
        z3 = score_matmul(kb - 2, 1)
        accumulate_rest(c2, kb - 1, stage_b, 0)
        score_rest(z2, stage_a, 0)
        accumulate_rest(c3, kb - 1, stage_b, 1)
        score_rest(z3, stage_a, 1)
        return carry

    lax.fori_loop(0, qi // 2, pair, 0)

    @pl.when(qi % 2 == 1)
    def _():
        c0 = cumsum_matmul(stage_a, 0)
        z0 = score_matmul(0, 0)
        c1 = cumsum_matmul(stage_a, 1)
        z1 = score_matmul(0, 1)
        accumulate_rest(c0, 1, stage_a, 0)
        score_rest(z0, stage_b, 0)
        c2 = cumsum_matmul(stage_b, 0)
        accumulate_rest(c1, 1, stage_a, 1)
        score_rest(z1, stage_b, 1)
        c3 = cumsum_matmul(stage_b, 1)
        accumulate_rest(c2, 0, stage_b, 0)
        accumulate_rest(c3, 0, stage_b, 1)

    @pl.when(qi % 2 == 0)
    def _():
        c0 = cumsum_matmul(stage_a, 0)
        c1 = cumsum_matmul(stage_a, 1)
        accumulate_rest(c0, 0, stage_a, 0)
        accumulate_rest(c1, 0, stage_a, 1)

    lane_o = lax.broadcasted_iota(jnp.int32, (tq, LANES), 1)
    o_ref[...] = jnp.where(lane_o < HEAD_DIM, acc_scr[0:tq, :], acc_scr[tq:2 * tq, :]).astype(BF16)


def _attention(qkv, batch, seq):
    t = qkv.shape[0]
    tq = ATT_T
    nq = seq // tq
    hp = D_MODEL // LANES
    j = lax.broadcasted_iota(jnp.int32, (tq, tq), 0)
    s = lax.broadcasted_iota(jnp.int32, (tq, tq), 1)
    tri = (j >= s).astype(BF16)
    return pl.pallas_call(
        _attn_kernel,
        grid=(batch, hp, nq),
        in_specs=[
            pl.BlockSpec((tq, LANES), lambda b, h, i: (b * nq + i, h)),
            pl.BlockSpec((seq, LANES), lambda b, h, i: (b, hp + h)),
            pl.BlockSpec((seq, LANES), lambda b, h, i: (b, 2 * hp + h)),
            _const_spec((tq, tq)),
        ],
        out_specs=pl.BlockSpec((tq, LANES), lambda b, h, i: (b * nq + i, h)),
        out_shape=jax.ShapeDtypeStruct((t, D_MODEL), BF16),
        scratch_shapes=[pltpu.VMEM((2 * tq, LANES), BF16), pltpu.VMEM((2 * tq, LANES), F32),
                        pltpu.VMEM((2 * tq, 1), F32)]
        + 2 * [pltpu.VMEM((2 * tq, tq), F32), pltpu.VMEM((2 * tq, tq), BF16), pltpu.VMEM((2 * tq, tq), BF16)],
        compiler_params=_params(3),
        name="stick_attn",
    )(qkv, qkv, qkv, tri)


def _attn_out_kernel(o_ref, h_ref, wo_ref, mg_ref, mb_ref, wrh_ref, wrl_ref, br_ref, y_ref, gates_ref):
    y = _ln(ALPHA * h_ref[...] + _dot(o_ref[...], wo_ref[...]), mg_ref[...], mb_ref[...])
    y_ref[...] = y
    yh = y.astype(BF16)
    yl = (y - yh.astype(F32)).astype(BF16)
    logits = _dot(yh, wrh_ref[...]) + _dot(yl, wrh_ref[...]) + _dot(yh, wrl_ref[...]) + br_ref[...]
    lane = lax.broadcasted_iota(jnp.int32, logits.shape, 1)
    m1 = jnp.max(logits, axis=-1, keepdims=True)
    i1 = jnp.min(jnp.where(logits == m1, lane, LANES), axis=-1, keepdims=True)
    rest = jnp.where(lane == i1, -jnp.inf, logits)
    m2 = jnp.max(rest, axis=-1, keepdims=True)
    i2 = jnp.min(jnp.where(rest == m2, lane, LANES), axis=-1, keepdims=True)
    e = jnp.exp(m2 - m1)
    w1 = 1.0 / (1.0 + e)
    w2 = e / (1.0 + e)
    gates_ref[...] = jnp.where(lane == i1, w1, 0.0) + jnp.where(lane == i2, w2, 0.0)


def _attn_out(o, h, wo, mg, mb, wr_hi, wr_lo, br):
    t, d = h.shape
    tm = OUT_TM
    row = lambda v: v.reshape(1, -1)
    return pl.pallas_call(
        _attn_out_kernel,
        grid=(t // tm,),
        in_specs=[
            pl.BlockSpec((tm, d), lambda i: (i, 0)),
            pl.BlockSpec((tm, d), lambda i: (i, 0)),
            _const_spec((d, d)), _const_spec((1, d)), _const_spec((1, d)),
            _const_spec((d, LANES)), _const_spec((d, LANES)), _const_spec((1, LANES)),
        ],
        out_specs=[pl.BlockSpec((tm, d), lambda i: (i, 0)), pl.BlockSpec((tm, LANES), lambda i: (i, 0))],
        out_shape=[jax.ShapeDtypeStruct((t, d), F32), jax.ShapeDtypeStruct((t, LANES), F32)],
        compiler_params=_params(1),
        name="attn_out_router",
    )(o, h, wo, row(mg), row(mb), wr_hi, wr_lo, br)


def _moe_kernel(h_ref, gates_ref, wg_ref, wu_ref, wd_ref, lg_ref, lb_ref, p_ref, wproj_ref, wpg_ref, bpg_ref,
                o_ref, acc_scr, hb_scr):
    e = pl.program_id(1)

    @pl.when(e == 0)
    def _():
        acc_scr[...] = jnp.zeros(acc_scr.shape, F32)
        hb_scr[...] = h_ref[...].astype(BF16)

    hb = hb_scr[...]
    gate = _dot(hb, wg_ref[...])
    up = _dot(hb, wu_ref[...])
    act = (gate * _sigmoid(gate) * up).astype(BF16)
    y = _dot(act, wd_ref[...])
    gates = gates_ref[...]
    lane = lax.broadcasted_iota(jnp.int32, gates.shape, 1)
    gcol = jnp.sum(jnp.where(lane == e, gates, 0.0), axis=-1, keepdims=True)
    acc_scr[...] = acc_scr[...] + gcol * y

    @pl.when(e == pl.num_programs(1) - 1)
    def _():
        o_ref[...] = _ffn_epilogue(h_ref[...], acc_scr[...], lg_ref, lb_ref, p_ref, wproj_ref, wpg_ref, bpg_ref)


def _moe(h, gates, wg, wu, wd, lg, lb, p3d, layer, wproj, wpg, bpg):
    t, d = h.shape
    ne, _, f = wg.shape
    dp = p3d.shape[2]
    tm = MOE_TM
    row = lambda v: v.reshape(1, -1)
    return pl.pallas_call(
        _moe_kernel,
        grid=(t // tm, ne),
        in_specs=[
            pl.BlockSpec((tm, d), lambda i, e: (i, 0)),
            pl.BlockSpec((tm, LANES), lambda i, e: (i, 0)),
            pl.BlockSpec((None, d, f), lambda i, e: (e, 0, 0)),
            pl.BlockSpec((None, d, f), lambda i, e: (e, 0, 0)),
            pl.BlockSpec((None, f, d), lambda i, e: (e, 0, 0)),
            _const_spec((1, d)), _const_spec((1, d)),
            pl.BlockSpec((None, tm, dp), lambda i, e: (layer, i, 0)),
            _const_spec((dp, d)), _const_spec((d, d)), _const_spec((1, d)),
        ],
        out_specs=pl.BlockSpec((tm, d), lambda i, e: (i, 0)),
        out_shape=jax.ShapeDtypeStruct((t, d), F32),
        scratch_shapes=[pltpu.VMEM((tm, d), F32), pltpu.VMEM((tm, d), BF16)],
        compiler_params=_params(2),
        name="moe_experts",
    )(h, gates, wg, wu, wd, row(lg), row(lb), p3d, wproj, wpg, row(bpg))


def kernel(x, p, conv_w_pw1, conv_b_pw1, conv_w_dw, conv_b_dw, conv_ln_g, conv_ln_b, conv_w_pw2, conv_b_pw2, attn_w_qkv, attn_w_o, ffn_w_gate, ffn_w_up, ffn_w_down, moe_w_router, moe_b_router, moe_w_gate, moe_w_up, moe_w_down, ln_mix_g, ln_mix_b, ln_ffn_g, ln_ffn_b, ple_w_proj, ple_w_gate, ple_b_gate):
    batch, seq, d = x.shape
    t = batch * seq
    bf = lambda w: w.astype(BF16)
    x2d = x.reshape(t, d)
    p3d = p.reshape(p.shape[0], t, p.shape[-1])

    h = _conv_mixer(x2d, batch, seq, bf(conv_w_pw1[0]), conv_b_pw1[0], conv_w_dw[0], conv_b_dw[0],
                    conv_ln_g[0], conv_ln_b[0], bf(conv_w_pw2[0]), conv_b_pw2[0], ln_mix_g[0], ln_mix_b[0])
    h = _ffn0(h, bf(ffn_w_gate[0]), bf(ffn_w_up[0]), bf(ffn_w_down[0]), ln_ffn_g[0], ln_ffn_b[0],
              p3d, 0, bf(ple_w_proj[0]), bf(ple_w_gate[0]), ple_b_gate[0])

    qkv = _qkv(h, bf(attn_w_qkv[0]))
    o = _attention(qkv, batch, seq)
    wr = jnp.zeros((d, LANES), F32).at[:, :N_EXPERTS].set(moe_w_router[0])
    wr_hi = wr.astype(BF16)
    wr_lo = (wr - wr_hi.astype(F32)).astype(BF16)
    br = jnp.full((1, LANES), -1e30, F32).at[0, :N_EXPERTS].set(moe_b_router[0])
    h, gates = _attn_out(o, h, bf(attn_w_o[0]), ln_mix_g[1], ln_mix_b[1], wr_hi, wr_lo, br)
    h = _moe(h, gates, bf(moe_w_gate[0]), bf(moe_w_up[0]), bf(moe_w_down[0]), ln_ffn_g[1], ln_ffn_b[1],
             p3d, 1, bf(ple_w_proj[1]), bf(ple_w_gate[1]), ple_b_gate[1])
    return h.reshape(batch, seq, d)
```

```python
import functools
import math

import jax
import jax.numpy as jnp
from jax import lax
from jax.experimental import pallas as pl
from jax.experimental.pallas import tpu as pltpu

F32 = jnp.float32
BF16 = jnp.bfloat16

D_MODEL = 1024
N_HEADS = 16
HEAD_DIM = D_MODEL // N_HEADS
CONV_WIDTH = 31
N_EXPERTS = 8
LN_EPS = 1e-5
DEPTH = 2
ALPHA = (2.0 * DEPTH) ** 0.25
LOG2E = math.log2(math.e)

LANES = 128
SUBLANES = 8
VMEM_LIMIT = 56 * 1024 * 1024

CONV_TM = 512
CONV_HALO = 32
FFN_TM = 512
QKV_TM = 512
ATT_T = 256
OUT_TM = 512
MOE_TILE = 512
DISPATCH_TB = 2048
COMBINE_TM = 512


def _ln(x, g, b):
    mu = jnp.mean(x, axis=-1, keepdims=True)
    xc = x - mu
    var = jnp.mean(xc * xc, axis=-1, keepdims=True)
    return xc * lax.rsqrt(var + LN_EPS) * g + b


def _sigmoid(x):
    return 1.0 / (1.0 + jnp.exp(-x))


def _dot(a, b):
    return jnp.dot(a, b, preferred_element_type=F32)


def _const_spec(shape):
    nd = len(shape)
    return pl.BlockSpec(shape, lambda *_: (0,) * nd, pipeline_mode=pl.Buffered(1))


def _params(n_axes):
    return pltpu.CompilerParams(dimension_semantics=("arbitrary",) * n_axes,
                                vmem_limit_bytes=VMEM_LIMIT)


def _dwconv(g_scr, wb_ref, bdw_ref, c_scr, tm):
    d = g_scr.shape[1]
    off = CONV_HALO - (CONV_WIDTH - 1)
    sub = lax.broadcasted_iota(jnp.int32, (SUBLANES, LANES), 0)
    masks = [sub >= r for r in range(SUBLANES)]
    n_a = (off + CONV_WIDTH - 1) // SUBLANES

    def valid(m):
        return off <= m < off + CONV_WIDTH

    for l in range(d // LANES):
        cols = slice(l * LANES, (l + 1) * LANES)

        def wv(m, cols=cols):
            return wb_ref[m - off, :, cols]

        def tile(j, cols=cols):
            if isinstance(j, int):
                return g_scr[j * SUBLANES:(j + 1) * SUBLANES, cols]
            return g_scr[pl.ds(pl.multiple_of(j * SUBLANES, SUBLANES), SUBLANES), cols]

        def shifted_sums(t, wv=wv):
            out = []
            for r in range(1, SUBLANES):
                acc = None
                for a in range(n_a):
                    m = SUBLANES * a + r
                    if valid(m):
                        term = wv(m) * t[a]
                        acc = term if acc is None else acc + term
                out.append(acc)
            return tuple(out)

        bias = jnp.broadcast_to(bdw_ref[:, cols], (SUBLANES, LANES))

        def body(i, carry, cols=cols, bias=bias, wv=wv, tile=tile, shifted_sums=shifted_sums):
            t = [tile(i + 1 + a) for a in range(n_a)]
            nxt = shifted_sums(t)
            acc = bias
            for a in range(1, n_a + 1):
                if valid(SUBLANES * a):
                    acc = acc + wv(SUBLANES * a) * t[a - 1]
            for r in range(1, SUBLANES):
                y = jnp.where(masks[r], carry[r - 1], nxt[r - 1])
                acc = acc + pltpu.roll(y, SUBLANES - r, 0)
            c_scr[pl.ds(pl.multiple_of(i * SUBLANES, SUBLANES), SUBLANES), cols] = acc
            return nxt

        lax.fori_loop(0, tm // SUBLANES, body, shifted_sums([tile(a) for a in range(n_a)]), unroll=4)


def _conv_mixer_kernel(x_ref, w1_ref, b1_ref, wb_ref, bdw_ref, lng_ref, lnb_ref, w2_ref, b2_ref,
                       mg_ref, mb_ref, o_ref, g_scr, c_scr):
    tm = x_ref.shape[0]
    d = x_ref.shape[1]

    @pl.when(pl.program_id(1) == 0)
    def _():
        g_scr[0:CONV_HALO, :] = jnp.zeros((CONV_HALO, d), F32)

    x = x_ref[...]
    h = _dot(x.astype(BF16), w1_ref[...]) + b1_ref[...]
    g_scr[CONV_HALO:CONV_HALO + tm, :] = h[:, :d] * _sigmoid(h[:, d:])
    _dwconv(g_scr, wb_ref, bdw_ref, c_scr, tm)
    g_scr[0:CONV_HALO, :] = g_scr[tm:tm + CONV_HALO, :]

    hc = _ln(c_scr[...], lng_ref[...], lnb_ref[...])
    hc = hc * _sigmoid(hc)
    mix = _dot(hc.astype(BF16), w2_ref[...]) + b2_ref[...]
    o_ref[...] = _ln(ALPHA * x + mix, mg_ref[...], mb_ref[...])


def _conv_mixer(x2d, batch, seq, w1, b1, wdw, bdw, lng, lnb, w2, b2, mg, mb):
    t, d = x2d.shape
    tm = CONV_TM
    ns = seq // tm
    row = lambda v: v.reshape(1, -1)
    return pl.pallas_call(
        _conv_mixer_kernel,
        grid=(batch, ns),
        in_specs=[
            pl.BlockSpec((tm, d), lambda b, s: (b * ns + s, 0)),
            _const_spec((d, 2 * d)), _const_spec((1, 2 * d)),
            _const_spec((CONV_WIDTH, SUBLANES, d)), _const_spec((1, d)),
            _const_spec((1, d)), _const_spec((1, d)),
            _const_spec((d, d)), _const_spec((1, d)),
            _const_spec((1, d)), _const_spec((1, d)),
        ],
        out_specs=pl.BlockSpec((tm, d), lambda b, s: (b * ns + s, 0)),
        out_shape=jax.ShapeDtypeStruct((t, d), F32),
        scratch_shapes=[pltpu.VMEM((CONV_HALO + tm, d), F32), pltpu.VMEM((tm, d), F32)],
        compiler_params=_params(2),
        name="conv_mixer",
    )(x2d, w1, row(b1), jnp.broadcast_to(wdw[:, None, :], (CONV_WIDTH, SUBLANES, d)), row(bdw),
      row(lng), row(lnb), w2, row(b2), row(mg), row(mb))


def _ffn_epilogue(h, ff, lg_ref, lb_ref, p_ref, wproj_ref, wpg_ref, bpg_ref):
    y = _ln(ALPHA * h + ff, lg_ref[...], lb_ref[...])
    gate = _sigmoid(_dot(y.astype(BF16), wpg_ref[...]) + bpg_ref[...])
    return y + gate * _dot(p_ref[...].astype(BF16), wproj_ref[...])


def _ffn0_kernel(h_ref, wg_ref, wu_ref, wd_ref, lg_ref, lb_ref, p_ref, wproj_ref, wpg_ref, bpg_ref, o_ref):
    h = h_ref[...]
    hb = h.astype(BF16)
    gate = _dot(hb, wg_ref[...])
    up = _dot(hb, wu_ref[...])
    act = (gate * _sigmoid(gate) * up).astype(BF16)
    ff = _dot(act, wd_ref[...])
    o_ref[...] = _ffn_epilogue(h, ff, lg_ref, lb_ref, p_ref, wproj_ref, wpg_ref, bpg_ref)


def _ffn0(h, wg, wu, wd, lg, lb, p3d, layer, wproj, wpg, bpg):
    t, d = h.shape
    f = wg.shape[1]
    dp = p3d.shape[2]
    tm = FFN_TM
    row = lambda v: v.reshape(1, -1)
    return pl.pallas_call(
        _ffn0_kernel,
        grid=(t // tm,),
        in_specs=[
            pl.BlockSpec((tm, d), lambda i: (i, 0)),
            _const_spec((d, f)), _const_spec((d, f)), _const_spec((f, d)),
            _const_spec((1, d)), _const_spec((1, d)),
            pl.BlockSpec((None, tm, dp), lambda i: (layer, i, 0)),
            _const_spec((dp, d)), _const_spec((d, d)), _const_spec((1, d)),
        ],
        out_specs=pl.BlockSpec((tm, d), lambda i: (i, 0)),
        out_shape=jax.ShapeDtypeStruct((t, d), F32),
        compiler_params=_params(1),
        name="ffn0",
    )(h, wg, wu, wd, row(lg), row(lb), p3d, wproj, wpg, row(bpg))


def _qkv_kernel(h_ref, w_ref, o_ref):
    d = h_ref.shape[1]
    qkv = _dot(h_ref[...].astype(BF16), w_ref[...])
    o_ref[:, :d] = (qkv[:, :d] * (HEAD_DIM ** -0.5 * LOG2E)).astype(BF16)
    o_ref[:, d:] = qkv[:, d:].astype(BF16)


def _qkv(h, w):
    t, d = h.shape
    n = w.shape[1]
    tm = QKV_TM
    return pl.pallas_call(
        _qkv_kernel,
        grid=(t // tm,),
        in_specs=[pl.BlockSpec((tm, d), lambda i: (i, 0)), _const_spec((d, n))],
        out_specs=pl.BlockSpec((tm, n), lambda i: (i, 0)),
        out_shape=jax.ShapeDtypeStruct((t, n), BF16),
        compiler_params=_params(1),
        name="qkv_proj",
    )(h, w)


def _attn_kernel(q_ref, k_ref, v_ref, tri_ref, o_ref, qs_scr, acc_scr, r_scr, *stage_scr):
    tq = q_ref.shape[0]
    qi = pl.program_id(2)
    stage_a, stage_b = stage_scr[:3], stage_scr[3:]

    nq = -q_ref[...]
    lane = lax.broadcasted_iota(jnp.int32, nq.shape, 1)
    zero = jnp.zeros_like(nq)
    qs_scr[0:tq, :] = jnp.where(lane < HEAD_DIM, nq, zero)
    qs_scr[tq:2 * tq, :] = jnp.where(lane >= HEAD_DIM, nq, zero)
    acc_scr[...] = jnp.zeros(acc_scr.shape, F32)
    r_scr[...] = jnp.zeros(r_scr.shape, F32)

    def rows(h):
        return slice(h * tq, (h + 1) * tq)

    def score_matmul(kb, h):
        start = pl.multiple_of(kb * tq, tq)
        return lax.dot_general(qs_scr[rows(h), :], k_ref[pl.ds(start, tq), :], (((1,), (1,)), ((), ())),
                               preferred_element_type=F32)

    def score_rest(zn, stage, h, masked=False):
        zn_scr, hi_scr, lo_scr = stage
        if masked:
            row = lax.broadcasted_iota(jnp.int32, zn.shape, 0)
            col = lax.broadcasted_iota(jnp.int32, zn.shape, 1)
            zn = jnp.where(col < row, zn, jnp.inf)
        log1m = jnp.minimum(zn, 0.0) - jnp.log(1.0 + jnp.exp2(-jnp.abs(zn))) * LOG2E
        hi = log1m.astype(BF16)
        zn_scr[rows(h), :] = zn
        hi_scr[rows(h), :] = hi
        lo_scr[rows(h), :] = (log1m - hi.astype(F32)).astype(BF16)

    def cumsum_matmul(stage, h):
        _, hi_scr, lo_scr = stage
        return _dot(hi_scr[rows(h), :], tri_ref[...]) + _dot(lo_scr[rows(h), :], tri_ref[...])

    def accumulate_rest(csum, kb, stage, h):
        start = pl.multiple_of(kb * tq, tq)
        r = r_scr[rows(h), :]
        w = jnp.exp2(csum + r - stage[0][rows(h), :])
        r_scr[rows(h), :] = r + csum[:, 0:1]
        acc_scr[rows(h), :] = acc_scr[rows(h), :] + _dot(w.astype(BF16), v_ref[pl.ds(start, tq), :])

    score_rest(score_matmul(qi, 0), stage_a, 0, masked=True)
    score_rest(score_matmul(qi, 1), stage_a, 1, masked=True)

    def pair(p, carry):
        kb = qi - 2 * p
        c0 = cumsum_matmul(stage_a, 0)
        z0 = score_matmul(kb - 1, 0)
        c1 = cumsum_matmul(stage_a, 1)
        z1 = score_matmul(kb - 1, 1)
        accumulate_rest(c0, kb, stage_a, 0)
        score_rest(z0, stage_b, 0)
        c2 = cumsum_matmul(stage_b, 0)
        z2 = score_matmul(kb - 2, 0)
        accumulate_rest(c1, kb, stage_a, 1)
        score_rest(z1, stage_b, 1)
        c3 = cumsum_matmul(stage_b, 1)
        z3 = score_matmul(kb - 2, 1)
        accumulate_rest(c2, kb - 1, stage_b, 0)
        score_rest(z2, stage_a, 0)
        accumulate_rest(c3, kb - 1, stage_b, 1)
        score_rest(z3, stage_a, 1)
        return carry

    lax.fori_loop(0, qi // 2, pair, 0)

    @pl.when(qi % 2 == 1)
    def _():
        c0 = cumsum_matmul(stage_a, 0)
        z0 = score_matmul(0, 0)
        c1 = cumsum_matmul(stage_a, 1)
        z1 = score_matmul(0, 1)
        accumulate_rest(c0, 1, stage_a, 0)
        score_rest(z0, stage_b, 0)
        c2 = cumsum_matmul(stage_b, 0)
        accumulate_rest(c1, 1, stage_a, 1)
        score_rest(z1, stage_b, 1)
        c3 = cumsum_matmul(stage_b, 1)
        accumulate_rest(c2, 0, stage_b, 0)
        accumulate_rest(c3, 0, stage_b, 1)

    @pl.when(qi % 2 == 0)
    def _():
        c0 = cumsum_matmul(stage_a, 0)
        c1 = cumsum_matmul(stage_a, 1)
        accumulate_rest(c0, 0, stage_a, 0)
        accumulate_rest(c1, 0, stage_a, 1)

    lane_o = lax.broadcasted_iota(jnp.int32, (tq, LANES), 1)
    o_ref[...] = jnp.where(lane_o < HEAD_DIM, acc_scr[0:tq, :], acc_scr[tq:2 * tq, :]).astype(BF16)


def _attention(qkv, batch, seq):
    t = qkv.shape[0]
    tq = ATT_T
    nq = seq // tq
    hp = D_MODEL // LANES
    j = lax.broadcasted_iota(jnp.int32, (tq, tq), 0)
    s = lax.broadcasted_iota(jnp.int32, (tq, tq), 1)
    tri = (j >= s).astype(BF16)
    return pl.pallas_call(
        _attn_kernel,
        grid=(batch, hp, nq),
        in_specs=[
            pl.BlockSpec((tq, LANES), lambda b, h, i: (b * nq + i, h)),
            pl.BlockSpec((seq, LANES), lambda b, h, i: (b, hp + h)),
            pl.BlockSpec((seq, LANES), lambda b, h, i: (b, 2 * hp + h)),
            _const_spec((tq, tq)),
        ],
        out_specs=pl.BlockSpec((tq, LANES), lambda b, h, i: (b * nq + i, h)),
        out_shape=jax.ShapeDtypeStruct((t, D_MODEL), BF16),
        scratch_shapes=[pltpu.VMEM((2 * tq, LANES), BF16), pltpu.VMEM((2 * tq, LANES), F32),
                        pltpu.VMEM((2 * tq, 1), F32)]
        + 2 * [pltpu.VMEM((2 * tq, tq), F32), pltpu.VMEM((2 * tq, tq), BF16), pltpu.VMEM((2 * tq, tq), BF16)],
        compiler_params=_params(3),
        name="stick_attn",
    )(qkv, qkv, qkv, tri)


ROUTE_E1, ROUTE_E2, ROUTE_W1, ROUTE_W2, ROUTE_R1, ROUTE_R2 = range(6)


def _attn_out_kernel(o_ref, h_ref, wo_ref, mg_ref, mb_ref, wrh_ref, wrl_ref, br_ref, ltri_ref,
                     y_ref, route_ref, cnt_ref, cnt_scr):
    @pl.when(pl.program_id(0) == 0)
    def _():
        cnt_scr[...] = jnp.zeros(cnt_scr.shape, F32)

    y = _ln(ALPHA * h_ref[...] + _dot(o_ref[...], wo_ref[...]), mg_ref[...], mb_ref[...])
    y_ref[...] = y
    yh = y.astype(BF16)
    yl = (y - yh.astype(F32)).astype(BF16)
    logits = _dot(yh, wrh_ref[...]) + _dot(yl, wrh_ref[...]) + _dot(yh, wrl_ref[...]) + br_ref[...]
    lane = lax.broadcasted_iota(jnp.int32, logits.shape, 1)
    m1 = jnp.max(logits, axis=-1, keepdims=True)
    i1 = jnp.min(jnp.where(logits == m1, lane, LANES), axis=-1, keepdims=True)
    rest = jnp.where(lane == i1, -jnp.inf, logits)
    m2 = jnp.max(rest, axis=-1, keepdims=True)
    i2 = jnp.min(jnp.where(rest == m2, lane, LANES), axis=-1, keepdims=True)
    e = jnp.exp(m2 - m1)
    w1 = 1.0 / (1.0 + e)
    w2 = e / (1.0 + e)
    oh1 = lane == i1
    oh2 = lane == i2
    sel = jnp.where(oh1 | oh2, 1.0, 0.0)
    before = _dot(ltri_ref[...], sel.astype(BF16)) + cnt_scr[...]
    rank1 = jnp.sum(jnp.where(oh1, before, 0.0), axis=-1, keepdims=True)
    rank2 = jnp.sum(jnp.where(oh2, before, 0.0), axis=-1, keepdims=True)
    cnt = cnt_scr[...] + jnp.sum(sel, axis=0, keepdims=True)
    cnt_scr[...] = cnt
    cnt_ref[...] = cnt
    route = jnp.zeros(logits.shape, F32)
    for c, v in ((ROUTE_E1, i1.astype(F32)), (ROUTE_E2, i2.astype(F32)), (ROUTE_W1, w1), (ROUTE_W2, w2),
                 (ROUTE_R1, rank1), (ROUTE_R2, rank2)):
        route = jnp.where(lane == c, v, route)
    route_ref[...] = route


def _attn_out(o, h, wo, mg, mb, wr_hi, wr_lo, br):
    t, d = h.shape
    tm = OUT_TM
    row = lambda v: v.reshape(1, -1)
    r = lax.broadcasted_iota(jnp.int32, (tm, tm), 0)
    c = lax.broadcasted_iota(jnp.int32, (tm, tm), 1)
    ltri = (c < r).astype(BF16)
    return pl.pallas_call(
        _attn_out_kernel,
        grid=(t // tm,),
        in_specs=[
            pl.BlockSpec((tm, d), lambda i: (i, 0)),
            pl.BlockSpec((tm, d), lambda i: (i, 0)),
            _const_spec((d, d)), _const_spec((1, d)), _const_spec((1, d)),
            _const_spec((d, LANES)), _const_spec((d, LANES)), _const_spec((1, LANES)),
            _const_spec((tm, tm)),
        ],
        out_specs=[pl.BlockSpec((tm, d), lambda i: (i, 0)), pl.BlockSpec((tm, LANES), lambda i: (i, 0)),
                   pl.BlockSpec((1, LANES), lambda i: (0, 0))],
        out_shape=[jax.ShapeDtypeStruct((t, d), F32), jax.ShapeDtypeStruct((t, LANES), F32),
                   jax.ShapeDtypeStruct((1, LANES), F32)],
        scratch_shapes=[pltpu.VMEM((1, LANES), F32)],
        compiler_params=_params(1),
        name="attn_out_router",
    )(o, h, wo, row(mg), row(mb), wr_hi, wr_lo, br, ltri)


def _route_plan(route, cnt, tile):
    t = route.shape[0]
    as_int = lambda lane: route[:, lane].astype(jnp.int32)
    counts = cnt[0, :N_EXPERTS].astype(jnp.int32)
    ends = jnp.cumsum(counts)
    starts = ends - counts
    experts = jnp.arange(N_EXPERTS, dtype=jnp.int32)

    def start_of(e):
        return jnp.sum(jnp.where(e[:, None] == experts[None, :], starts[None, :], 0), axis=1)

    dest1 = start_of(as_int(ROUTE_E1)) + as_int(ROUTE_R1)
    dest2 = start_of(as_int(ROUTE_E2)) + as_int(ROUTE_R2)

    n_tiles = 2 * t // tile
    n_items = n_tiles + N_EXPERTS - 1
    tile_lo = jnp.arange(n_tiles, dtype=jnp.int32)[:, None] * tile
    lo = jnp.maximum(tile_lo, starts[None, :])
    hi = jnp.minimum(tile_lo + tile, ends[None, :])
    valid = (hi > lo).reshape(-1)
    n_valid = jnp.sum(valid.astype(jnp.int32))
    (idx,) = jnp.nonzero(valid, size=n_items, fill_value=0)
    idx = idx.astype(jnp.int32)
    is_pad = jnp.arange(n_items, dtype=jnp.int32) >= n_valid
    idx = jnp.where(is_pad, idx[n_valid - 1], idx)
    it_tile = idx // N_EXPERTS
    it_exp = idx % N_EXPERTS
    it_lo = jnp.where(is_pad, 0, lo.reshape(-1)[idx] - it_tile * tile)
    it_hi = jnp.where(is_pad, 0, hi.reshape(-1)[idx] - it_tile * tile)
    it_first = jnp.concatenate([jnp.ones((1,), jnp.int32), (it_tile[1:] != it_tile[:-1]).astype(jnp.int32)])
    return dest1, dest2, (it_tile, it_exp, it_lo, it_hi, it_first)


def _row_copy(src_hbm, src_row, dst, dst_row, sem):
    return pltpu.make_async_copy(src_hbm.at[pl.ds(src_row, 1), :], dst.at[pl.ds(dst_row, 1), :], sem)


def _dispatch_kernel(d1_ref, d2_ref, h_hbm, xs_hbm, sem):
    base = pl.program_id(0) * DISPATCH_TB

    def copies(j):
        t = base + j
        return (_row_copy(h_hbm, t, xs_hbm, d1_ref[t], sem), _row_copy(h_hbm, t, xs_hbm, d2_ref[t], sem))

    def start(j, carry):
        for cp in copies(j):
            cp.start()
        return carry

    def wait(j, carry):
        for cp in copies(j):
            cp.wait()
        return carry

    lax.fori_loop(0, DISPATCH_TB, start, 0, unroll=8)
    lax.fori_loop(0, DISPATCH_TB, wait, 0, unroll=8)


def _dispatch(h, dest1, dest2):
    t, d = h.shape
    return pl.pallas_call(
        _dispatch_kernel,
        grid_spec=pltpu.PrefetchScalarGridSpec(
            num_scalar_prefetch=2,
            grid=(t // DISPATCH_TB,),
            in_specs=[pl.BlockSpec(memory_space=pl.ANY)],
            out_specs=pl.BlockSpec(memory_space=pl.ANY),
            scratch_shapes=[pltpu.SemaphoreType.DMA(())],
        ),
        out_shape=jax.ShapeDtypeStruct((2 * t, d), F32),
        compiler_params=_params(1),
        name="moe_dispatch",
    )(dest1, dest2, h)


def _grouped_kernel(tile_ref, exp_ref, lo_ref, hi_ref, first_ref, x_ref, wg_ref, wu_ref, wd_ref, o_ref):
    i = pl.program_id(0)
    lo = lo_ref[i]
    hi = hi_ref[i]

    @pl.when(hi > lo)
    def _():
        xb = x_ref[...].astype(BF16)
        gate = _dot(xb, wg_ref[...])
        up = _dot(xb, wu_ref[...])
        act = (gate * _sigmoid(gate) * up).astype(BF16)
        y = _dot(act, wd_ref[...])
        row = lax.broadcasted_iota(jnp.int32, y.shape, 0)
        mine = (row >= lo) & (row < hi)

        @pl.when(first_ref[i] == 1)
        def _():
            o_ref[...] = jnp.where(mine, y, 0.0)

        @pl.when(first_ref[i] == 0)
        def _():
            o_ref[...] = jnp.where(mine, y, o_ref[...])


def _grouped_experts(xs, items, wg, wu, wd):
    r, d = xs.shape
    ne, _, f = wg.shape
    tile = MOE_TILE
    n_items = items[0].shape[0]
    return pl.pallas_call(
        _grouped_kernel,
        grid_spec=pltpu.PrefetchScalarGridSpec(
            num_scalar_prefetch=5,
            grid=(n_items,),
            in_specs=[
                pl.BlockSpec((tile, d), lambda i, tl, ex, lo, hi, fi: (tl[i], 0)),
                pl.BlockSpec((None, d, f), lambda i, tl, ex, lo, hi, fi: (ex[i], 0, 0)),
                pl.BlockSpec((None, d, f), lambda i, tl, ex, lo, hi, fi: (ex[i], 0, 0)),
                pl.BlockSpec((None, f, d), lambda i, tl, ex, lo, hi, fi: (ex[i], 0, 0)),
            ],
            out_specs=pl.BlockSpec((tile, d), lambda i, tl, ex, lo, hi, fi: (tl[i], 0)),
        ),
        out_shape=jax.ShapeDtypeStruct((r, d), F32),
        compiler_params=_params(1),
        name="moe_grouped",
    )(*items, xs, wg, wu, wd)


def _combine_kernel(d1_ref, d2_ref, ys_hbm, h_ref, route_ref, lg_ref, lb_ref, p_ref, wproj_ref, wpg_ref, bpg_ref,
                    o_ref, y_scr, sem):
    tm = h_ref.shape[0]
    base = pl.program_id(0) * tm

    def copies(j):
        t = base + j
        return (_row_copy(ys_hbm, d1_ref[t], y_scr.at[0], j, sem), _row_copy(ys_hbm, d2_ref[t], y_scr.at[1], j, sem))

    def start(j, carry):
        for cp in copies(j):
            cp.start()
        return carry

    def wait(j, carry):
        for cp in copies(j):
            cp.wait()
        return carry

    lax.fori_loop(0, tm, start, 0, unroll=8)
    lax.fori_loop(0, tm, wait, 0, unroll=8)

    route = route_ref[...]
    lane = lax.broadcasted_iota(jnp.int32, route.shape, 1)
    w1 = jnp.sum(jnp.where(lane == ROUTE_W1, route, 0.0), axis=-1, keepdims=True)
    w2 = jnp.sum(jnp.where(lane == ROUTE_W2, route, 0.0), axis=-1, keepdims=True)
    ff = w1 * y_scr[0] + w2 * y_scr[1]
    o_ref[...] = _ffn_epilogue(h_ref[...], ff, lg_ref, lb_ref, p_ref, wproj_ref, wpg_ref, bpg_ref)


def _combine(ys, dest1, dest2, h, route, lg, lb, p3d, layer, wproj, wpg, bpg):
    t, d = h.shape
    dp = p3d.shape[2]
    tm = COMBINE_TM
    row = lambda v: v.reshape(1, -1)
    const = lambda shape: pl.BlockSpec(shape, lambda i, d1, d2: (0,) * len(shape), pipeline_mode=pl.Buffered(1))
    return pl.pallas_call(
        _combine_kernel,
        grid_spec=pltpu.PrefetchScalarGridSpec(
            num_scalar_prefetch=2,
            grid=(t // tm,),
            in_specs=[
                pl.BlockSpec(memory_space=pl.ANY),
                pl.BlockSpec((tm, d), lambda i, d1, d2: (i, 0)),
                pl.BlockSpec((tm, LANES), lambda i, d1, d2: (i, 0)),
                const((1, d)), const((1, d)),
                pl.BlockSpec((None, tm, dp), lambda i, d1, d2: (layer, i, 0)),
                const((dp, d)), const((d, d)), const((1, d)),
            ],
            out_specs=pl.BlockSpec((tm, d), lambda i, d1, d2: (i, 0)),
            scratch_shapes=[pltpu.VMEM((2, tm, d), F32), pltpu.SemaphoreType.DMA(())],
        ),
        out_shape=jax.ShapeDtypeStruct((t, d), F32),
        compiler_params=_params(1),
        name="moe_combine",
    )(dest1, dest2, ys, h, route, row(lg), row(lb), p3d, wproj, wpg, row(bpg))


def _moe(h, route, cnt, wg, wu, wd, lg, lb, p3d, layer, wproj, wpg, bpg):
    dest1, dest2, items = _route_plan(route, cnt, MOE_TILE)
    xs = _dispatch(h, dest1, dest2)
    ys = _grouped_experts(xs, items, wg, wu, wd)
    return _combine(ys, dest1, dest2, h, route, lg, lb, p3d, layer, wproj, wpg, bpg)


def kernel(x, p, conv_w_pw1, conv_b_pw1, conv_w_dw, conv_b_dw, conv_ln_g, conv_ln_b, conv_w_pw2, conv_b_pw2, attn_w_qkv, attn_w_o, ffn_w_gate, ffn_w_up, ffn_w_down, moe_w_router, moe_b_router, moe_w_gate, moe_w_up, moe_w_down, ln_mix_g, ln_mix_b, ln_ffn_g, ln_ffn_b, ple_w_proj, ple_w_gate, ple_b_gate):
    batch, seq, d = x.shape
    t = batch * seq
    bf = lambda w: w.astype(BF16)
    x2d = x.reshape(t, d)
    p3d = p.reshape(p.shape[0], t, p.shape[-1])

    h = _conv_mixer(x2d, batch, seq, bf(conv_w_pw1[0]), conv_b_pw1[0], conv_w_dw[0], conv_b_dw[0],
                    conv_ln_g[0], conv_ln_b[0], bf(conv_w_pw2[0]), conv_b_pw2[0], ln_mix_g[0], ln_mix_b[0])
    h = _ffn0(h, bf(ffn_w_gate[0]), bf(ffn_w_up[0]), bf(ffn_w_down[0]), ln_ffn_g[0], ln_ffn_b[0],
              p3d, 0, bf(ple_w_proj[0]), bf(ple_w_gate[0]), ple_b_gate[0])

    qkv = _qkv(h, bf(attn_w_qkv[0]))
    o = _attention(qkv, batch, seq)
    wr = jnp.zeros((d, LANES), F32).at[:, :N_EXPERTS].set(moe_w_router[0])
    wr_hi = wr.astype(BF16)
    wr_lo = (wr - wr_hi.astype(F32)).astype(BF16)
    br = jnp.full((1, LANES), -1e30, F32).at[0, :N_EXPERTS].set(moe_b_router[0])
    h, route, cnt = _attn_out(o, h, bf(attn_w_o[0]), ln_mix_g[1], ln_mix_b[1], wr_hi, wr_lo, br)
    h = _moe(h, route, cnt, bf(moe_w_gate[0]), bf(moe_w_up[0]), bf(moe_w_down[0]), ln_ffn_g[1], ln_ffn_b[1],
             p3d, 1, bf(ple_w_proj[1]), bf(ple_w_gate[1]), ple_b_gate[1])
    return h.reshape(batch, seq, d)
```

```python
import functools
import math

import jax
import jax.numpy as jnp
from jax import lax
from jax.experimental import pallas as pl
from jax.experimental.pallas import tpu as pltpu

F32 = jnp.float32
BF16 = jnp.bfloat16

D_MODEL = 1024
N_HEADS = 16
HEAD_DIM = D_MODEL // N_HEADS
CONV_WIDTH = 31
N_EXPERTS = 8
LN_EPS = 1e-5
DEPTH = 2
ALPHA = (2.0 * DEPTH) ** 0.25
LOG2E = math.log2(math.e)

LANES = 128
SUBLANES = 8
VMEM_LIMIT = 56 * 1024 * 1024

CONV_TM = 512
CONV_HALO = 32
FFN_TM = 512
QKV_TM = 512
ATT_T = 256
OUT_TM = 512
MOE_TILE = 512
DISPATCH_TB = 512
COMBINE_TM = 512


def _ln(x, g, b):
    mu = jnp.mean(x, axis=-1, keepdims=True)
    xc = x - mu
    var = jnp.mean(xc * xc, axis=-1, keepdims=True)
    return xc * lax.rsqrt(var + LN_EPS) * g + b


def _sigmoid(x):
    return 1.0 / (1.0 + jnp.exp(-x))


def _dot(a, b):
    return jnp.dot(a, b, preferred_element_type=F32)


def _const_spec(shape):
    nd = len(shape)
    return pl.BlockSpec(shape, lambda *_: (0,) * nd, pipeline_mode=pl.Buffered(1))


def _params(n_axes):
    return pltpu.CompilerParams(dimension_semantics=("arbitrary",) * n_axes,
                                vmem_limit_bytes=VMEM_LIMIT)


def _dwconv(g_scr, wb_ref, bdw_ref, c_scr, tm):
    d = g_scr.shape[1]
    off = CONV_HALO - (CONV_WIDTH - 1)
    sub = lax.broadcasted_iota(jnp.int32, (SUBLANES, LANES), 0)
    masks = [sub >= r for r in range(SUBLANES)]
    n_a = (off + CONV_WIDTH - 1) // SUBLANES

    def valid(m):
        return off <= m < off + CONV_WIDTH

    for l in range(d // LANES):
        cols = slice(l * LANES, (l + 1) * LANES)

        def wv(m, cols=cols):
            return wb_ref[m - off, :, cols]

        def tile(j, cols=cols):
            if isinstance(j, int):
                return g_scr[j * SUBLANES:(j + 1) * SUBLANES, cols]
            return g_scr[pl.ds(pl.multiple_of(j * SUBLANES, SUBLANES), SUBLANES), cols]

        def shifted_sums(t, wv=wv):
            out = []
            for r in range(1, SUBLANES):
                acc = None
                for a in range(n_a):
                    m = SUBLANES * a + r
                    if valid(m):
                        term = wv(m) * t[a]
                        acc = term if acc is None else acc + term
                out.append(acc)
            return tuple(out)

        bias = jnp.broadcast_to(bdw_ref[:, cols], (SUBLANES, LANES))

        def body(i, carry, cols=cols, bias=bias, wv=wv, tile=tile, shifted_sums=shifted_sums):
            t = [tile(i + 1 + a) for a in range(n_a)]
            nxt = shifted_sums(t)
            acc = bias
            for a in range(1, n_a + 1):
                if valid(SUBLANES * a):
                    acc = acc + wv(SUBLANES * a) * t[a - 1]
            for r in range(1, SUBLANES):
                y = jnp.where(masks[r], carry[r - 1], nxt[r - 1])
                acc = acc + pltpu.roll(y, SUBLANES - r, 0)
            c_scr[pl.ds(pl.multiple_of(i * SUBLANES, SUBLANES), SUBLANES), cols] = acc
            return nxt

        lax.fori_loop(0, tm // SUBLANES, body, shifted_sums([tile(a) for a in range(n_a)]), unroll=4)


def _conv_mixer_kernel(x_ref, w1_ref, b1_ref, wb_ref, bdw_ref, lng_ref, lnb_ref, w2_ref, b2_ref,
                       mg_ref, mb_ref, o_ref, g_scr, c_scr):
    tm = x_ref.shape[0]
    d = x_ref.shape[1]

    @pl.when(pl.program_id(1) == 0)
    def _():
        g_scr[0:CONV_HALO, :] = jnp.zeros((CONV_HALO, d), F32)

    x = x_ref[...]
    h = _dot(x.astype(BF16), w1_ref[...]) + b1_ref[...]
    g_scr[CONV_HALO:CONV_HALO + tm, :] = h[:, :d] * _sigmoid(h[:, d:])
    _dwconv(g_scr, wb_ref, bdw_ref, c_scr, tm)
    g_scr[0:CONV_HALO, :] = g_scr[tm:tm + CONV_HALO, :]

    hc = _ln(c_scr[...], lng_ref[...], lnb_ref[...])
    hc = hc * _sigmoid(hc)
    mix = _dot(hc.astype(BF16), w2_ref[...]) + b2_ref[...]
    o_ref[...] = _ln(ALPHA * x + mix, mg_ref[...], mb_ref[...])


def _conv_mixer(x2d, batch, seq, w1, b1, wdw, bdw, lng, lnb, w2, b2, mg, mb):
    t, d = x2d.shape
    tm = CONV_TM
    ns = seq // tm
    row = lambda v: v.reshape(1, -1)
    return pl.pallas_call(
        _conv_mixer_kernel,
        grid=(batch, ns),
        in_specs=[
            pl.BlockSpec((tm, d), lambda b, s: (b * ns + s, 0)),
            _const_spec((d, 2 * d)), _const_spec((1, 2 * d)),
            _const_spec((CONV_WIDTH, SUBLANES, d)), _const_spec((1, d)),
            _const_spec((1, d)), _const_spec((1, d)),
            _const_spec((d, d)), _const_spec((1, d)),
            _const_spec((1, d)), _const_spec((1, d)),
        ],
        out_specs=pl.BlockSpec((tm, d), lambda b, s: (b * ns + s, 0)),
        out_shape=jax.ShapeDtypeStruct((t, d), F32),
        scratch_shapes=[pltpu.VMEM((CONV_HALO + tm, d), F32), pltpu.VMEM((tm, d), F32)],
        compiler_params=_params(2),
        name="conv_mixer",
    )(x2d, w1, row(b1), jnp.broadcast_to(wdw[:, None, :], (CONV_WIDTH, SUBLANES, d)), row(bdw),
      row(lng), row(lnb), w2, row(b2), row(mg), row(mb))


def _ffn_epilogue(h, ff, lg_ref, lb_ref, p_ref, wproj_ref, wpg_ref, bpg_ref):
    y = _ln(ALPHA * h + ff, lg_ref[...], lb_ref[...])
    gate = _sigmoid(_dot(y.astype(BF16), wpg_ref[...]) + bpg_ref[...])
    return y + gate * _dot(p_ref[...].astype(BF16), wproj_ref[...])


def _ffn0_kernel(h_ref, wg_ref, wu_ref, wd_ref, lg_ref, lb_ref, p_ref, wproj_ref, wpg_ref, bpg_ref, o_ref):
    h = h_ref[...]
    hb = h.astype(BF16)
    gate = _dot(hb, wg_ref[...])
    up = _dot(hb, wu_ref[...])
    act = (gate * _sigmoid(gate) * up).astype(BF16)
    ff = _dot(act, wd_ref[...])
    o_ref[...] = _ffn_epilogue(h, ff, lg_ref, lb_ref, p_ref, wproj_ref, wpg_ref, bpg_ref)


def _ffn0(h, wg, wu, wd, lg, lb, p3d, layer, wproj, wpg, bpg):
    t, d = h.shape
    f = wg.shape[1]
    dp = p3d.shape[2]
    tm = FFN_TM
    row = lambda v: v.reshape(1, -1)
    return pl.pallas_call(
        _ffn0_kernel,
        grid=(t // tm,),
        in_specs=[
            pl.BlockSpec((tm, d), lambda i: (i, 0)),
            _const_spec((d, f)), _const_spec((d, f)), _const_spec((f, d)),
            _const_spec((1, d)), _const_spec((1, d)),
            pl.BlockSpec((None, tm, dp), lambda i: (layer, i, 0)),
            _const_spec((dp, d)), _const_spec((d, d)), _const_spec((1, d)),
        ],
        out_specs=pl.BlockSpec((tm, d), lambda i: (i, 0)),
        out_shape=jax.ShapeDtypeStruct((t, d), F32),
        compiler_params=_params(1),
        name="ffn0",
    )(h, wg, wu, wd, row(lg), row(lb), p3d, wproj, wpg, row(bpg))


def _qkv_kernel(h_ref, w_ref, o_ref):
    d = h_ref.shape[1]
    qkv = _dot(h_ref[...].astype(BF16), w_ref[...])
    o_ref[:, :d] = (qkv[:, :d] * (HEAD_DIM ** -0.5 * LOG2E)).astype(BF16)
    o_ref[:, d:] = qkv[:, d:].astype(BF16)


def _qkv(h, w):
    t, d = h.shape
    n = w.shape[1]
    tm = QKV_TM
    return pl.pallas_call(
        _qkv_kernel,
        grid=(t // tm,),
        in_specs=[pl.BlockSpec((tm, d), lambda i: (i, 0)), _const_spec((d, n))],
        out_specs=pl.BlockSpec((tm, n), lambda i: (i, 0)),
        out_shape=jax.ShapeDtypeStruct((t, n), BF16),
        compiler_params=_params(1),
        name="qkv_proj",
    )(h, w)


def _attn_kernel(q_ref, k_ref, v_ref, tri_ref, o_ref, qs_scr, acc_scr, r_scr, *stage_scr):
    tq = q_ref.shape[0]
    qi = pl.program_id(2)
    stage_a, stage_b = stage_scr[:2], stage_scr[2:]

    nq = -q_ref[...]
    lane = lax.broadcasted_iota(jnp.int32, nq.shape, 1)
    zero = jnp.zeros_like(nq)
    qs_scr[0:tq, :] = jnp.where(lane < HEAD_DIM, nq, zero)
    qs_scr[tq:2 * tq, :] = jnp.where(lane >= HEAD_DIM, nq, zero)
    acc_scr[...] = jnp.zeros(acc_scr.shape, F32)
    r_scr[...] = jnp.zeros(r_scr.shape, F32)

    def rows(h):
        return slice(h * tq, (h + 1) * tq)

    def score_matmul(kb, h):
        start = pl.multiple_of(kb * tq, tq)
        return lax.dot_general(qs_scr[rows(h), :], k_ref[pl.ds(start, tq), :], (((1,), (1,)), ((), ())),
                               preferred_element_type=F32)

    def score_rest(zn, stage, h, masked=False):
        zn_scr, l_scr = stage
        if masked:
            row = lax.broadcasted_iota(jnp.int32, zn.shape, 0)
            col = lax.broadcasted_iota(jnp.int32, zn.shape, 1)
            zn = jnp.where(col < row, zn, jnp.inf)
        log1m = jnp.minimum(zn, 0.0) - jnp.log(1.0 + jnp.exp2(-jnp.abs(zn))) * LOG2E
        zn_scr[rows(h), :] = zn
        l_scr[rows(h), :] = log1m.astype(BF16)

    def cumsum_matmul(stage, h):
        return _dot(stage[1][rows(h), :], tri_ref[...])

    def accumulate_rest(csum, kb, stage, h):
        start = pl.multiple_of(kb * tq, tq)
        r = r_scr[rows(h), :]
        w = jnp.exp2(csum + r - stage[0][rows(h), :])
        r_scr[rows(h), :] = r + csum[:, 0:1]
        acc_scr[rows(h), :] = acc_scr[rows(h), :] + _dot(w.astype(BF16), v_ref[pl.ds(start, tq), :])

    score_rest(score_matmul(qi, 0), stage_a, 0, masked=True)
    score_rest(score_matmul(qi, 1), stage_a, 1, masked=True)

    def pair(p, carry):
        kb = qi - 2 * p
        c0 = cumsum_matmul(stage_a, 0)
        z0 = score_matmul(kb - 1, 0)
        c1 = cumsum_matmul(stage_a, 1)
        z1 = score_matmul(kb - 1, 1)
        accumulate_rest(c0, kb, stage_a, 0)
        score_rest(z0, stage_b, 0)
        c2 = cumsum_matmul(stage_b, 0)
        z2 = score_matmul(kb - 2, 0)
        accumulate_rest(c1, kb, stage_a, 1)
        score_rest(z1, stage_b, 1)
        c3 = cumsum_matmul(stage_b, 1)
        z3 = score_matmul(kb - 2, 1)
        accumulate_rest(c2, kb - 1, stage_b, 0)
        score_rest(z2, stage_a, 0)
        accumulate_rest(c3, kb - 1, stage_b, 1)
        score_rest(z3, stage_a, 1)
        return carry

    lax.fori_loop(0, qi // 2, pair, 0)

    @pl.when(qi % 2 == 1)
    def _():
        c0 = cumsum_matmul(stage_a, 0)
        z0 = score_matmul(0, 0)
        c1 = cumsum_matmul(stage_a, 1)
        z1 = score_matmul(0, 1)
        accumulate_rest(c0, 1, stage_a, 0)
        score_rest(z0, stage_b, 0)
        c2 = cumsum_matmul(stage_b, 0)
        accumulate_rest(c1, 1, stage_a, 1)
        score_rest(z1, stage_b, 1)
        c3 = cumsum_matmul(stage_b, 1)
        accumulate_rest(c2, 0, stage_b, 0)
        accumulate_rest(c3, 0, stage_b, 1)

    @pl.when(qi % 2 == 0)
    def _():
        c0 = cumsum_matmul(stage_a, 0)
        c1 = cumsum_matmul(stage_a, 1)
        accumulate_rest(c0, 0, stage_a, 0)
        accumulate_rest(c1, 0, stage_a, 1)

    lane_o = lax.broadcasted_iota(jnp.int32, (tq, LANES), 1)
    o_ref[...] = jnp.where(lane_o < HEAD_DIM, acc_scr[0:tq, :], acc_scr[tq:2 * tq, :]).astype(BF16)


def _attention(qkv, batch, seq):
    t = qkv.shape[0]
    tq = ATT_T
    nq = seq // tq
    hp = D_MODEL // LANES
    j = lax.broadcasted_iota(jnp.int32, (tq, tq), 0)
    s = lax.broadcasted_iota(jnp.int32, (tq, tq), 1)
    tri = (j >= s).astype(BF16)
    return pl.pallas_call(
        _attn_kernel,
        grid=(batch, hp, nq),
        in_specs=[
            pl.BlockSpec((tq, LANES), lambda b, h, i: (b * nq + i, h)),
            pl.BlockSpec((seq, LANES), lambda b, h, i: (b, hp + h)),
            pl.BlockSpec((seq, LANES), lambda b, h, i: (b, 2 * hp + h)),
            _const_spec((tq, tq)),
        ],
        out_specs=pl.BlockSpec((tq, LANES), lambda b, h, i: (b * nq + i, h)),
        out_shape=jax.ShapeDtypeStruct((t, D_MODEL), BF16),
        scratch_shapes=[pltpu.VMEM((2 * tq, LANES), BF16), pltpu.VMEM((2 * tq, LANES), F32),
                        pltpu.VMEM((2 * tq, 1), F32)]
        + 2 * [pltpu.VMEM((2 * tq, tq), F32), pltpu.VMEM((2 * tq, tq), BF16)],
        compiler_params=_params(3),
        name="stick_attn",
    )(qkv, qkv, qkv, tri)


ROUTE_E1, ROUTE_E2, ROUTE_W1, ROUTE_W2, ROUTE_R1, ROUTE_R2 = range(6)


def _attn_out_kernel(o_ref, h_ref, wo_ref, mg_ref, mb_ref, wrh_ref, wrl_ref, br_ref, ltri_ref,
                     y_ref, route_ref, cnt_ref, cnt_scr):
    @pl.when(pl.program_id(0) == 0)
    def _():
        cnt_scr[...] = jnp.zeros(cnt_scr.shape, F32)

    y = _ln(ALPHA * h_ref[...] + _dot(o_ref[...], wo_ref[...]), mg_ref[...], mb_ref[...])
    y_ref[...] = y
    yh = y.astype(BF16)
    yl = (y - yh.astype(F32)).astype(BF16)
    logits = _dot(yh, wrh_ref[...]) + _dot(yl, wrh_ref[...]) + _dot(yh, wrl_ref[...]) + br_ref[...]
    lane = lax.broadcasted_iota(jnp.int32, logits.shape, 1)
    m1 = jnp.max(logits, axis=-1, keepdims=True)
    i1 = jnp.min(jnp.where(logits == m1, lane, LANES), axis=-1, keepdims=True)
    rest = jnp.where(lane == i1, -jnp.inf, logits)
    m2 = jnp.max(rest, axis=-1, keepdims=True)
    i2 = jnp.min(jnp.where(rest == m2, lane, LANES), axis=-1, keepdims=True)
    e = jnp.exp(m2 - m1)
    w1 = 1.0 / (1.0 + e)
    w2 = e / (1.0 + e)
    oh1 = lane == i1
    oh2 = lane == i2
    sel = jnp.where(oh1 | oh2, 1.0, 0.0)
    before = _dot(ltri_ref[...], sel.astype(BF16)) + cnt_scr[...]
    rank1 = jnp.sum(jnp.where(oh1, before, 0.0), axis=-1, keepdims=True)
    rank2 = jnp.sum(jnp.where(oh2, before, 0.0), axis=-1, keepdims=True)
    cnt = cnt_scr[...] + jnp.sum(sel, axis=0, keepdims=True)
    cnt_scr[...] = cnt
    cnt_ref[...] = cnt
    route = jnp.zeros(logits.shape, F32)
    for c, v in ((ROUTE_E1, i1.astype(F32)), (ROUTE_E2, i2.astype(F32)), (ROUTE_W1, w1), (ROUTE_W2, w2),
                 (ROUTE_R1, rank1), (ROUTE_R2, rank2)):
        route = jnp.where(lane == c, v, route)
    route_ref[...] = route


def _attn_out(o, h, wo, mg, mb, wr_hi, wr_lo, br):
    t, d = h.shape
    tm = OUT_TM
    row = lambda v: v.reshape(1, -1)
    r = lax.broadcasted_iota(jnp.int32, (tm, tm), 0)
    c = lax.broadcasted_iota(jnp.int32, (tm, tm), 1)
    ltri = (c < r).astype(BF16)
    return pl.pallas_call(
        _attn_out_kernel,
        grid=(t // tm,),
        in_specs=[
            pl.BlockSpec((tm, d), lambda i: (i, 0)),
            pl.BlockSpec((tm, d), lambda i: (i, 0)),
            _const_spec((d, d)), _const_spec((1, d)), _const_spec((1, d)),
            _const_spec((d, LANES)), _const_spec((d, LANES)), _const_spec((1, LANES)),
            _const_spec((tm, tm)),
        ],
        out_specs=[pl.BlockSpec((tm, d), lambda i: (i, 0)), pl.BlockSpec((tm, LANES), lambda i: (i, 0)),
                   pl.BlockSpec((1, LANES), lambda i: (0, 0))],
        out_shape=[jax.ShapeDtypeStruct((t, d), F32), jax.ShapeDtypeStruct((t, LANES), F32),
                   jax.ShapeDtypeStruct((1, LANES), F32)],
        scratch_shapes=[pltpu.VMEM((1, LANES), F32)],
        compiler_params=_params(1),
        name="attn_out_router",
    )(o, h, wo, row(mg), row(mb), wr_hi, wr_lo, br, ltri)


def _route_plan(route, cnt, tile):
    t = route.shape[0]
    as_int = lambda lane: route[:, lane].astype(jnp.int32)
    counts = cnt[0, :N_EXPERTS].astype(jnp.int32)
    ends = jnp.cumsum(counts)
    starts = ends - counts
    experts = jnp.arange(N_EXPERTS, dtype=jnp.int32)

    def start_of(e):
        return jnp.sum(jnp.where(e[:, None] == experts[None, :], starts[None, :], 0), axis=1)

    dest1 = start_of(as_int(ROUTE_E1)) + as_int(ROUTE_R1)
    dest2 = start_of(as_int(ROUTE_E2)) + as_int(ROUTE_R2)

    n_tiles = 2 * t // tile
    n_items = n_tiles + N_EXPERTS - 1
    tile_lo = jnp.arange(n_tiles, dtype=jnp.int32)[:, None] * tile
    lo = jnp.maximum(tile_lo, starts[None, :])
    hi = jnp.minimum(tile_lo + tile, ends[None, :])
    valid = (hi > lo).reshape(-1)
    n_valid = jnp.sum(valid.astype(jnp.int32))
    (idx,) = jnp.nonzero(valid, size=n_items, fill_value=0)
    idx = idx.astype(jnp.int32)
    is_pad = jnp.arange(n_items, dtype=jnp.int32) >= n_valid
    idx = jnp.where(is_pad, idx[n_valid - 1], idx)
    it_tile = idx // N_EXPERTS
    it_exp = idx % N_EXPERTS
    it_lo = jnp.where(is_pad, 0, lo.reshape(-1)[idx] - it_tile * tile)
    it_hi = jnp.where(is_pad, 0, hi.reshape(-1)[idx] - it_tile * tile)
    it_first = jnp.concatenate([jnp.ones((1,), jnp.int32), (it_tile[1:] != it_tile[:-1]).astype(jnp.int32)])
    return dest1, dest2, (it_tile, it_exp, it_lo, it_hi, it_first)


def _row_copy(src, src_row, dst, dst_row, sem):
    return pltpu.make_async_copy(src.at[pl.ds(src_row, 1), :], dst.at[pl.ds(dst_row, 1), :], sem)


def _dispatch_kernel(d1_ref, d2_ref, h_ref, xs_hbm, sem):
    base = pl.program_id(0) * DISPATCH_TB

    def copies(j):
        t = base + j
        return (_row_copy(h_ref, j, xs_hbm, d1_ref[t], sem), _row_copy(h_ref, j, xs_hbm, d2_ref[t], sem))

    def start(j, carry):
        for cp in copies(j):
            cp.start()
        return carry

    def wait(j, carry):
        for cp in copies(j):
            cp.wait()
        return carry

    lax.fori_loop(0, DISPATCH_TB, start, 0, unroll=8)
    lax.fori_loop(0, DISPATCH_TB, wait, 0, unroll=8)


def _dispatch(h, dest1, dest2):
    t, d = h.shape
    return pl.pallas_call(
        _dispatch_kernel,
        grid_spec=pltpu.PrefetchScalarGridSpec(
            num_scalar_prefetch=2,
            grid=(t // DISPATCH_TB,),
            in_specs=[pl.BlockSpec((DISPATCH_TB, d), lambda i, d1, d2: (i, 0))],
            out_specs=pl.BlockSpec(memory_space=pl.ANY),
            scratch_shapes=[pltpu.SemaphoreType.DMA(())],
        ),
        out_shape=jax.ShapeDtypeStruct((2 * t, d), F32),
        compiler_params=_params(1),
        name="moe_dispatch",
    )(dest1, dest2, h)


def _grouped_kernel(tile_ref, exp_ref, lo_ref, hi_ref, first_ref, x_ref, wg_ref, wu_ref, wd_ref, o_ref):
    i = pl.program_id(0)
    lo = lo_ref[i]
    hi = hi_ref[i]

    @pl.when(hi > lo)
    def _():
        xb = x_ref[...].astype(BF16)
        gate = _dot(xb, wg_ref[...])
        up = _dot(xb, wu_ref[...])
        act = (gate * _sigmoid(gate) * up).astype(BF16)
        y = _dot(act, wd_ref[...])
        row = lax.broadcasted_iota(jnp.int32, y.shape, 0)
        mine = (row >= lo) & (row < hi)

        @pl.when(first_ref[i] == 1)
        def _():
            o_ref[...] = jnp.where(mine, y, 0.0)

        @pl.when(first_ref[i] == 0)
        def _():
            o_ref[...] = jnp.where(mine, y, o_ref[...])


def _grouped_experts(xs, items, wg, wu, wd):
    r, d = xs.shape
    ne, _, f = wg.shape
    tile = MOE_TILE
    n_items = items[0].shape[0]
    return pl.pallas_call(
        _grouped_kernel,
        grid_spec=pltpu.PrefetchScalarGridSpec(
            num_scalar_prefetch=5,
            grid=(n_items,),
            in_specs=[
                pl.BlockSpec((tile, d), lambda i, tl, ex, lo, hi, fi: (tl[i], 0)),
                pl.BlockSpec((None, d, f), lambda i, tl, ex, lo, hi, fi: (ex[i], 0, 0)),
                pl.BlockSpec((None, d, f), lambda i, tl, ex, lo, hi, fi: (ex[i], 0, 0)),
                pl.BlockSpec((None, f, d), lambda i, tl, ex, lo, hi, fi: (ex[i], 0, 0)),
            ],
            out_specs=pl.BlockSpec((tile, d), lambda i, tl, ex, lo, hi, fi: (tl[i], 0)),
        ),
        out_shape=jax.ShapeDtypeStruct((r, d), F32),
        compiler_params=_params(1),
        name="moe_grouped",
    )(*items, xs, wg, wu, wd)


def _combine_kernel(d1_ref, d2_ref, ys_hbm, h_ref, route_ref, lg_ref, lb_ref, p_ref, wproj_ref, wpg_ref, bpg_ref,
                    o_ref, y_scr, sem):
    tm = h_ref.shape[0]
    base = pl.program_id(0) * tm

    def copies(j):
        t = base + j
        return (_row_copy(ys_hbm, d1_ref[t], y_scr.at[0], j, sem), _row_copy(ys_hbm, d2_ref[t], y_scr.at[1], j, sem))

    def start(j, carry):
        for cp in copies(j):
            cp.start()
        return carry

    def wait(j, carry):
        for cp in copies(j):
            cp.wait()
        return carry

    lax.fori_loop(0, tm, start, 0, unroll=8)
    lax.fori_loop(0, tm, wait, 0, unroll=8)

    route = route_ref[...]
    lane = lax.broadcasted_iota(jnp.int32, route.shape, 1)
    w1 = jnp.sum(jnp.where(lane == ROUTE_W1, route, 0.0), axis=-1, keepdims=True)
    w2 = jnp.sum(jnp.where(lane == ROUTE_W2, route, 0.0), axis=-1, keepdims=True)
    ff = w1 * y_scr[0] + w2 * y_scr[1]
    o_ref[...] = _ffn_epilogue(h_ref[...], ff, lg_ref, lb_ref, p_ref, wproj_ref, wpg_ref, bpg_ref)


def _combine(ys, dest1, dest2, h, route, lg, lb, p3d, layer, wproj, wpg, bpg):
    t, d = h.shape
    dp = p3d.shape[2]
    tm = COMBINE_TM
    row = lambda v: v.reshape(1, -1)
    const = lambda shape: pl.BlockSpec(shape, lambda i, d1, d2: (0,) * len(shape), pipeline_mode=pl.Buffered(1))
    return pl.pallas_call(
        _combine_kernel,
        grid_spec=pltpu.PrefetchScalarGridSpec(
            num_scalar_prefetch=2,
            grid=(t // tm,),
            in_specs=[
                pl.BlockSpec(memory_space=pl.ANY),
                pl.BlockSpec((tm, d), lambda i, d1, d2: (i, 0)),
                pl.BlockSpec((tm, LANES), lambda i, d1, d2: (i, 0)),
                const((1, d)), const((1, d)),
                pl.BlockSpec((None, tm, dp), lambda i, d1, d2: (layer, i, 0)),
                const((dp, d)), const((d, d)), const((1, d)),
            ],
            out_specs=pl.BlockSpec((tm, d), lambda i, d1, d2: (i, 0)),
            scratch_shapes=[pltpu.VMEM((2, tm, d), F32), pltpu.SemaphoreType.DMA(())],
        ),
        out_shape=jax.ShapeDtypeStruct((t, d), F32),
        compiler_params=_params(1),
        name="moe_combine",
    )(dest1, dest2, ys, h, route, row(lg), row(lb), p3d, wproj, wpg, row(bpg))


def _moe(h, route, cnt, wg, wu, wd, lg, lb, p3d, layer, wproj, wpg, bpg):
    dest1, dest2, items = _route_plan(route, cnt, MOE_TILE)
    xs = _dispatch(h, dest1, dest2)
    ys = _grouped_experts(xs, items, wg, wu, wd)
    return _combine(ys, dest1, dest2, h, route, lg, lb, p3d, layer, wproj, wpg, bpg)


def kernel(x, p, conv_w_pw1, conv_b_pw1, conv_w_dw, conv_b_dw, conv_ln_g, conv_ln_b, conv_w_pw2, conv_b_pw2, attn_w_qkv, attn_w_o, ffn_w_gate, ffn_w_up, ffn_w_down, moe_w_router, moe_b_router, moe_w_gate, moe_w_up, moe_w_down, ln_mix_g, ln_mix_b, ln_ffn_g, ln_ffn_b, ple_w_proj, ple_w_gate, ple_b_gate):
    batch, seq, d = x.shape
    t = batch * seq
    bf = lambda w: w.astype(BF16)
    x2d = x.reshape(t, d)
    p3d = p.reshape(p.shape[0], t, p.shape[-1])

    h = _conv_mixer(x2d, batch, seq, bf(conv_w_pw1[0]), conv_b_pw1[0], conv_w_dw[0], conv_b_dw[0],
                    conv_ln_g[0], conv_ln_b[0], bf(conv_w_pw2[0]), conv_b_pw2[0], ln_mix_g[0], ln_mix_b[0])
    h = _ffn0(h, bf(ffn_w_gate[0]), bf(ffn_w_up[0]), bf(ffn_w_down[0]), ln_ffn_g[0], ln_ffn_b[0],
              p3d, 0, bf(ple_w_proj[0]), bf(ple_w_gate[0]), ple_b_gate[0])

    qkv = _qkv(h, bf(attn_w_qkv[0]))
    o = _attention(qkv, batch, seq)
    wr = jnp.zeros((d, LANES), F32).at[:, :N_EXPERTS].set(moe_w_router[0])
    wr_hi = wr.astype(BF16)
    wr_lo = (wr - wr_hi.astype(F32)).astype(BF16)
    br = jnp.full((1, LANES), -1e30, F32).at[0, :N_EXPERTS].set(moe_b_router[0])
    h, route, cnt = _attn_out(o, h, bf(attn_w_o[0]), ln_mix_g[1], ln_mix_b[1], wr_hi, wr_lo, br)
    h = _moe(h, route, cnt, bf(moe_w_gate[0]), bf(moe_w_up[0]), bf(moe_w_down[0]), ln_ffn_g[1], ln_ffn_b[1],
             p3d, 1, bf(ple_w_proj[1]), bf(ple_w_gate[1]), ple_b_gate[1])
    return h.reshape(batch, seq, d)
```

```python
import functools
import math

import jax
import jax.numpy as jnp
from jax import lax
from jax.experimental import pallas as pl
from jax.experimental.pallas import tpu as pltpu

F32 = jnp.float32
BF16 = jnp.bfloat16

D_MODEL = 1024
N_HEADS = 16
HEAD_DIM = D_MODEL // N_HEADS
CONV_WIDTH = 31
N_EXPERTS = 8
LN_EPS = 1e-5
DEPTH = 2
ALPHA = (2.0 * DEPTH) ** 0.25
LOG2E = math.log2(math.e)

LANES = 128
SUBLANES = 8
VMEM_LIMIT = 56 * 1024 * 1024

CONV_TM = 512
CONV_HALO = 32
FFN_TM = 512
QKV_TM = 512
ATT_T = 256
OUT_TM = 512
MOE_TILE = 512
DISPATCH_TB = 512
COMBINE_TM = 512


def _ln(x, g, b):
    mu = jnp.mean(x, axis=-1, keepdims=True)
    xc = x - mu
    var = jnp.mean(xc * xc, axis=-1, keepdims=True)
    return xc * lax.rsqrt(var + LN_EPS) * g + b


def _sigmoid(x):
    return 1.0 / (1.0 + jnp.exp(-x))


def _dot(a, b):
    return jnp.dot(a, b, preferred_element_type=F32)


def _const_spec(shape):
    nd = len(shape)
    return pl.BlockSpec(shape, lambda *_: (0,) * nd, pipeline_mode=pl.Buffered(1))


def _params(n_axes):
    return pltpu.CompilerParams(dimension_semantics=("arbitrary",) * n_axes,
                                vmem_limit_bytes=VMEM_LIMIT)


def _dwconv(g_scr, wb_ref, bdw_ref, c_scr, tm):
    d = g_scr.shape[1]
    off = CONV_HALO - (CONV_WIDTH - 1)
    sub = lax.broadcasted_iota(jnp.int32, (SUBLANES, LANES), 0)
    masks = [sub >= r for r in range(SUBLANES)]
    n_a = (off + CONV_WIDTH - 1) // SUBLANES

    def valid(m):
        return off <= m < off + CONV_WIDTH

    for l in range(d // LANES):
        cols = slice(l * LANES, (l + 1) * LANES)

        def wv(m, cols=cols):
            return wb_ref[m - off, :, cols]

        def tile(j, cols=cols):
            if isinstance(j, int):
                return g_scr[j * SUBLANES:(j + 1) * SUBLANES, cols]
            return g_scr[pl.ds(pl.multiple_of(j * SUBLANES, SUBLANES), SUBLANES), cols]

        def shifted_sums(t, wv=wv):
            out = []
            for r in range(1, SUBLANES):
                acc = None
                for a in range(n_a):
                    m = SUBLANES * a + r
                    if valid(m):
                        term = wv(m) * t[a]
                        acc = term if acc is None else acc + term
                out.append(acc)
            return tuple(out)

        bias = jnp.broadcast_to(bdw_ref[:, cols], (SUBLANES, LANES))

        def body(i, carry, cols=cols, bias=bias, wv=wv, tile=tile, shifted_sums=shifted_sums):
            t = [tile(i + 1 + a) for a in range(n_a)]
            nxt = shifted_sums(t)
            acc = bias
            for a in range(1, n_a + 1):
                if valid(SUBLANES * a):
                    acc = acc + wv(SUBLANES * a) * t[a - 1]
            for r in range(1, SUBLANES):
                y = jnp.where(masks[r], carry[r - 1], nxt[r - 1])
                acc = acc + pltpu.roll(y, SUBLANES - r, 0)
            c_scr[pl.ds(pl.multiple_of(i * SUBLANES, SUBLANES), SUBLANES), cols] = acc
            return nxt

        lax.fori_loop(0, tm // SUBLANES, body, shifted_sums([tile(a) for a in range(n_a)]), unroll=4)


def _conv_mixer_kernel(x_ref, w1_ref, b1_ref, wb_ref, bdw_ref, lng_ref, lnb_ref, w2_ref, b2_ref,
                       mg_ref, mb_ref, o_ref, g_scr, c_scr):
    tm = x_ref.shape[0]
    d = x_ref.shape[1]

    @pl.when(pl.program_id(1) == 0)
    def _():
        g_scr[0:CONV_HALO, :] = jnp.zeros((CONV_HALO, d), F32)

    x = x_ref[...]
    h = _dot(x.astype(BF16), w1_ref[...]) + b1_ref[...]
    g_scr[CONV_HALO:CONV_HALO + tm, :] = h[:, :d] * _sigmoid(h[:, d:])
    _dwconv(g_scr, wb_ref, bdw_ref, c_scr, tm)
    g_scr[0:CONV_HALO, :] = g_scr[tm:tm + CONV_HALO, :]

    hc = _ln(c_scr[...], lng_ref[...], lnb_ref[...])
    hc = hc * _sigmoid(hc)
    mix = _dot(hc.astype(BF16), w2_ref[...]) + b2_ref[...]
    o_ref[...] = _ln(ALPHA * x + mix, mg_ref[...], mb_ref[...])


def _conv_mixer(x2d, batch, seq, w1, b1, wdw, bdw, lng, lnb, w2, b2, mg, mb):
    t, d = x2d.shape
    tm = CONV_TM
    ns = seq // tm
    row = lambda v: v.reshape(1, -1)
    return pl.pallas_call(
        _conv_mixer_kernel,
        grid=(batch, ns),
        in_specs=[
            pl.BlockSpec((tm, d), lambda b, s: (b * ns + s, 0)),
            _const_spec((d, 2 * d)), _const_spec((1, 2 * d)),
            _const_spec((CONV_WIDTH, SUBLANES, d)), _const_spec((1, d)),
            _const_spec((1, d)), _const_spec((1, d)),
            _const_spec((d, d)), _const_spec((1, d)),
            _const_spec((1, d)), _const_spec((1, d)),
        ],
        out_specs=pl.BlockSpec((tm, d), lambda b, s: (b * ns + s, 0)),
        out_shape=jax.ShapeDtypeStruct((t, d), F32),
        scratch_shapes=[pltpu.VMEM((CONV_HALO + tm, d), F32), pltpu.VMEM((tm, d), F32)],
        compiler_params=_params(2),
        name="conv_mixer",
    )(x2d, w1, row(b1), jnp.broadcast_to(wdw[:, None, :], (CONV_WIDTH, SUBLANES, d)), row(bdw),
      row(lng), row(lnb), w2, row(b2), row(mg), row(mb))


def _ffn_epilogue(h, ff, lg_ref, lb_ref, p_ref, wproj_ref, wpg_ref, bpg_ref):
    y = _ln(ALPHA * h + ff, lg_ref[...], lb_ref[...])
    gate = _sigmoid(_dot(y.astype(BF16), wpg_ref[...]) + bpg_ref[...])
    return y + gate * _dot(p_ref[...].astype(BF16), wproj_ref[...])


def _ffn0_kernel(h_ref, wg_ref, wu_ref, wd_ref, lg_ref, lb_ref, p_ref, wproj_ref, wpg_ref, bpg_ref, o_ref):
    h = h_ref[...]
    hb = h.astype(BF16)
    gate = _dot(hb, wg_ref[...])
    up = _dot(hb, wu_ref[...])
    act = (gate * _sigmoid(gate) * up).astype(BF16)
    ff = _dot(act, wd_ref[...])
    o_ref[...] = _ffn_epilogue(h, ff, lg_ref, lb_ref, p_ref, wproj_ref, wpg_ref, bpg_ref)


def _ffn0(h, wg, wu, wd, lg, lb, p3d, layer, wproj, wpg, bpg):
    t, d = h.shape
    f = wg.shape[1]
    dp = p3d.shape[2]
    tm = FFN_TM
    row = lambda v: v.reshape(1, -1)
    return pl.pallas_call(
        _ffn0_kernel,
        grid=(t // tm,),
        in_specs=[
            pl.BlockSpec((tm, d), lambda i: (i, 0)),
            _const_spec((d, f)), _const_spec((d, f)), _const_spec((f, d)),
            _const_spec((1, d)), _const_spec((1, d)),
            pl.BlockSpec((None, tm, dp), lambda i: (layer, i, 0)),
            _const_spec((dp, d)), _const_spec((d, d)), _const_spec((1, d)),
        ],
        out_specs=pl.BlockSpec((tm, d), lambda i: (i, 0)),
        out_shape=jax.ShapeDtypeStruct((t, d), F32),
        compiler_params=_params(1),
        name="ffn0",
    )(h, wg, wu, wd, row(lg), row(lb), p3d, wproj, wpg, row(bpg))


def _qkv_kernel(h_ref, w_ref, o_ref):
    d = h_ref.shape[1]
    qkv = _dot(h_ref[...].astype(BF16), w_ref[...])
    o_ref[:, :d] = (qkv[:, :d] * (HEAD_DIM ** -0.5 * LOG2E)).astype(BF16)
    o_ref[:, d:] = qkv[:, d:].astype(BF16)


def _qkv(h, w):
    t, d = h.shape
    n = w.shape[1]
    tm = QKV_TM
    return pl.pallas_call(
        _qkv_kernel,
        grid=(t // tm,),
        in_specs=[pl.BlockSpec((tm, d), lambda i: (i, 0)), _const_spec((d, n))],
        out_specs=pl.BlockSpec((tm, n), lambda i: (i, 0)),
        out_shape=jax.ShapeDtypeStruct((t, n), BF16),
        compiler_params=_params(1),
        name="qkv_proj",
    )(h, w)


def _attn_kernel(q_ref, k_ref, v_ref, tri_ref, o_ref, qs_scr, acc_scr, r_scr, *stage_scr):
    tq = ATT_T
    n_q = q_ref.shape[0] // tq
    stages = (stage_scr[:2], stage_scr[2:])
    items = [(qi, kb, h) for qi in range(n_q) for kb in range(qi, -1, -1) for h in (0, 1)]

    def rows(h):
        return slice(h * tq, (h + 1) * tq)

    def block(i):
        return slice(i * tq, (i + 1) * tq)

    lane = lax.broadcasted_iota(jnp.int32, (tq, LANES), 1)
    for qi in range(n_q):
        nq = -q_ref[block(qi), :]
        zero = jnp.zeros_like(nq)
        qs_scr[qi, rows(0), :] = jnp.where(lane < HEAD_DIM, nq, zero)
        qs_scr[qi, rows(1), :] = jnp.where(lane >= HEAD_DIM, nq, zero)

    def score_matmul(n):
        qi, kb, h = items[n]
        return lax.dot_general(qs_scr[qi, rows(h), :], k_ref[block(kb), :], (((1,), (1,)), ((), ())),
                               preferred_element_type=F32)

    def score_rest(zn, n):
        qi, kb, h = items[n]
        zn_scr, l_scr = stages[(n // 2) % 2]
        if kb == qi:
            row = lax.broadcasted_iota(jnp.int32, zn.shape, 0)
            col = lax.broadcasted_iota(jnp.int32, zn.shape, 1)
            zn = jnp.where(col < row, zn, jnp.inf)
        log1m = jnp.minimum(zn, 0.0) - jnp.log(1.0 + jnp.exp2(-jnp.abs(zn))) * LOG2E
        zn_scr[rows(h), :] = zn
        l_scr[rows(h), :] = log1m.astype(BF16)

    def cumsum_matmul(n):
        h = items[n][2]
        return _dot(stages[(n // 2) % 2][1][rows(h), :], tri_ref[...])

    def accumulate_rest(csum, n):
        qi, kb, h = items[n]
        zn = stages[(n // 2) % 2][0][rows(h), :]
        if kb == qi:
            w = jnp.exp2(csum - zn)
            carried = csum[:, 0:1]
        else:
            r = r_scr[rows(h), :]
            w = jnp.exp2(csum + r - zn)
            carried = r + csum[:, 0:1]
        acc = _dot(w.astype(BF16), v_ref[block(kb), :])
        if kb != qi:
            acc = acc_scr[rows(h), :] + acc
        if kb != 0:
            r_scr[rows(h), :] = carried
            acc_scr[rows(h), :] = acc
        elif h == 0:
            acc_scr[rows(h), :] = acc
        else:
            o_ref[block(qi), :] = jnp.where(lane < HEAD_DIM, acc_scr[rows(0), :], acc).astype(BF16)

    n_items = len(items)
    score_rest(score_matmul(0), 0)
    score_rest(score_matmul(1), 1)
    csums, scores = {}, {}
    for n in range(n_items):
        csums[n] = cumsum_matmul(n)
        if n + 2 < n_items:
            scores[n + 2] = score_matmul(n + 2)
        if n >= 1:
            accumulate_rest(csums.pop(n - 1), n - 1)
            if n + 1 < n_items:
                score_rest(scores.pop(n + 1), n + 1)
    accumulate_rest(csums.pop(n_items - 1), n_items - 1)


def _attention(qkv, batch, seq):
    t = qkv.shape[0]
    tq = ATT_T
    nq = seq // tq
    hp = D_MODEL // LANES
    j = lax.broadcasted_iota(jnp.int32, (tq, tq), 0)
    s = lax.broadcasted_iota(jnp.int32, (tq, tq), 1)
    tri = (j >= s).astype(BF16)
    return pl.pallas_call(
        _attn_kernel,
        grid=(batch, hp),
        in_specs=[
            pl.BlockSpec((seq, LANES), lambda b, h: (b, h)),
            pl.BlockSpec((seq, LANES), lambda b, h: (b, hp + h)),
            pl.BlockSpec((seq, LANES), lambda b, h: (b, 2 * hp + h)),
            _const_spec((tq, tq)),
        ],
        out_specs=pl.BlockSpec((seq, LANES), lambda b, h: (b, h)),
        out_shape=jax.ShapeDtypeStruct((t, D_MODEL), BF16),
        scratch_shapes=[pltpu.VMEM((nq, 2 * tq, LANES), BF16), pltpu.VMEM((2 * tq, LANES), F32),
                        pltpu.VMEM((2 * tq, 1), F32)]
        + 2 * [pltpu.VMEM((2 * tq, tq), F32), pltpu.VMEM((2 * tq, tq), BF16)],
        compiler_params=_params(2),
        name="stick_attn",
    )(qkv, qkv, qkv, tri)


ROUTE_E1, ROUTE_E2, ROUTE_W1, ROUTE_W2, ROUTE_R1, ROUTE_R2 = range(6)


def _attn_out_kernel(o_ref, h_ref, wo_ref, mg_ref, mb_ref, wrh_ref, wrl_ref, br_ref, ltri_ref,
                     y_ref, route_ref, cnt_ref, cnt_scr):
    @pl.when(pl.program_id(0) == 0)
    def _():
        cnt_scr[...] = jnp.zeros(cnt_scr.shape, F32)

    y = _ln(ALPHA * h_ref[...] + _dot(o_ref[...], wo_ref[...]), mg_ref[...], mb_ref[...])
    y_ref[...] = y
    yh = y.astype(BF16)
    yl = (y - yh.astype(F32)).astype(BF16)
    logits = _dot(yh, wrh_ref[...]) + _dot(yl, wrh_ref[...]) + _dot(yh, wrl_ref[...]) + br_ref[...]
    lane = lax.broadcasted_iota(jnp.int32, logits.shape, 1)
    m1 = jnp.max(logits, axis=-1, keepdims=True)
    i1 = jnp.min(jnp.where(logits == m1, lane, LANES), axis=-1, keepdims=True)
    rest = jnp.where(lane == i1, -jnp.inf, logits)
    m2 = jnp.max(rest, axis=-1, keepdims=True)
    i2 = jnp.min(jnp.where(rest == m2, lane, LANES), axis=-1, keepdims=True)
    e = jnp.exp(m2 - m1)
    w1 = 1.0 / (1.0 + e)
    w2 = e / (1.0 + e)
    oh1 = lane == i1
    oh2 = lane == i2
    sel = jnp.where(oh1 | oh2, 1.0, 0.0)
    before = _dot(ltri_ref[...], sel.astype(BF16)) + cnt_scr[...]
    rank1 = jnp.sum(jnp.where(oh1, before, 0.0), axis=-1, keepdims=True)
    rank2 = jnp.sum(jnp.where(oh2, before, 0.0), axis=-1, keepdims=True)
    cnt = cnt_scr[...] + jnp.sum(sel, axis=0, keepdims=True)
    cnt_scr[...] = cnt
    cnt_ref[...] = cnt
    route = jnp.zeros(logits.shape, F32)
    for c, v in ((ROUTE_E1, i1.astype(F32)), (ROUTE_E2, i2.astype(F32)), (ROUTE_W1, w1), (ROUTE_W2, w2),
                 (ROUTE_R1, rank1), (ROUTE_R2, rank2)):
        route = jnp.where(lane == c, v, route)
    route_ref[...] = route


def _attn_out(o, h, wo, mg, mb, wr_hi, wr_lo, br):
    t, d = h.shape
    tm = OUT_TM
    row = lambda v: v.reshape(1, -1)
    r = lax.broadcasted_iota(jnp.int32, (tm, tm), 0)
    c = lax.broadcasted_iota(jnp.int32, (tm, tm), 1)
    ltri = (c < r).astype(BF16)
    return pl.pallas_call(
        _attn_out_kernel,
        grid=(t // tm,),
        in_specs=[
            pl.BlockSpec((tm, d), lambda i: (i, 0)),
            pl.BlockSpec((tm, d), lambda i: (i, 0)),
            _const_spec((d, d)), _const_spec((1, d)), _const_spec((1, d)),
            _const_spec((d, LANES)), _const_spec((d, LANES)), _const_spec((1, LANES)),
            _const_spec((tm, tm)),
        ],
        out_specs=[pl.BlockSpec((tm, d), lambda i: (i, 0)), pl.BlockSpec((tm, LANES), lambda i: (i, 0)),
                   pl.BlockSpec((1, LANES), lambda i: (0, 0))],
        out_shape=[jax.ShapeDtypeStruct((t, d), F32), jax.ShapeDtypeStruct((t, LANES), F32),
                   jax.ShapeDtypeStruct((1, LANES), F32)],
        scratch_shapes=[pltpu.VMEM((1, LANES), F32)],
        compiler_params=_params(1),
        name="attn_out_router",
    )(o, h, wo, row(mg), row(mb), wr_hi, wr_lo, br, ltri)


def _route_plan(route, cnt, tile):
    t = route.shape[0]
    as_int = lambda lane: route[:, lane].astype(jnp.int32)
    counts = cnt[0, :N_EXPERTS].astype(jnp.int32)
    ends = jnp.cumsum(counts)
    starts = ends - counts
    dest1 = jnp.take(starts, as_int(ROUTE_E1)) + as_int(ROUTE_R1)
    dest2 = jnp.take(starts, as_int(ROUTE_E2)) + as_int(ROUTE_R2)

    n_tiles = 2 * t // tile
    n_items = n_tiles + N_EXPERTS - 1
    tile_lo = jnp.arange(n_tiles, dtype=jnp.int32)[:, None] * tile
    lo = jnp.maximum(tile_lo, starts[None, :])
    hi = jnp.minimum(tile_lo + tile, ends[None, :])
    valid = (hi > lo).reshape(-1)
    n_valid = jnp.sum(valid.astype(jnp.int32))
    (idx,) = jnp.nonzero(valid, size=n_items, fill_value=0)
    idx = idx.astype(jnp.int32)
    is_pad = jnp.arange(n_items, dtype=jnp.int32) >= n_valid
    idx = jnp.where(is_pad, idx[n_valid - 1], idx)
    it_tile = idx // N_EXPERTS
    it_exp = idx % N_EXPERTS
    it_lo = jnp.where(is_pad, 0, lo.reshape(-1)[idx] - it_tile * tile)
    it_hi = jnp.where(is_pad, 0, hi.reshape(-1)[idx] - it_tile * tile)
    it_first = jnp.concatenate([jnp.ones((1,), jnp.int32), (it_tile[1:] != it_tile[:-1]).astype(jnp.int32)])
    return dest1, dest2, (it_tile, it_exp, it_lo, it_hi, it_first)


def _row_copy(src, src_row, dst, dst_row, sem):
    return pltpu.make_async_copy(src.at[pl.ds(src_row, 1), :], dst.at[pl.ds(dst_row, 1), :], sem)


def _dispatch_kernel(d1_ref, d2_ref, h_ref, xs_hbm, sem):
    base = pl.program_id(0) * DISPATCH_TB

    def copies(j):
        t = base + j
        return (_row_copy(h_ref, j, xs_hbm, d1_ref[t], sem), _row_copy(h_ref, j, xs_hbm, d2_ref[t], sem))

    def start(j, carry):
        for cp in copies(j):
            cp.start()
        return carry

    def wait(j, carry):
        for cp in copies(j):
            cp.wait()
        return carry

    lax.fori_loop(0, DISPATCH_TB, start, 0, unroll=8)
    lax.fori_loop(0, DISPATCH_TB, wait, 0, unroll=8)


def _dispatch(h, dest1, dest2):
    t, d = h.shape
    return pl.pallas_call(
        _dispatch_kernel,
        grid_spec=pltpu.PrefetchScalarGridSpec(
            num_scalar_prefetch=2,
            grid=(t // DISPATCH_TB,),
            in_specs=[pl.BlockSpec((DISPATCH_TB, d), lambda i, d1, d2: (i, 0))],
            out_specs=pl.BlockSpec(memory_space=pl.ANY),
            scratch_shapes=[pltpu.SemaphoreType.DMA(())],
        ),
        out_shape=jax.ShapeDtypeStruct((2 * t, d), F32),
        compiler_params=_params(1),
        name="moe_dispatch",
    )(dest1, dest2, h)


def _grouped_kernel(tile_ref, exp_ref, lo_ref, hi_ref, first_ref, x_ref, wg_ref, wu_ref, wd_ref, o_ref):
    i = pl.program_id(0)
    lo = lo_ref[i]
    hi = hi_ref[i]

    @pl.when(hi > lo)
    def _():
        xb = x_ref[...].astype(BF16)
        gate = _dot(xb, wg_ref[...])
        up = _dot(xb, wu_ref[...])
        act = (gate * _sigmoid(gate) * up).astype(BF16)
        y = _dot(act, wd_ref[...])
        row = lax.broadcasted_iota(jnp.int32, y.shape, 0)
        mine = (row >= lo) & (row < hi)

        @pl.when(first_ref[i] == 1)
        def _():
            o_ref[...] = jnp.where(mine, y, 0.0)

        @pl.when(first_ref[i] == 0)
        def _():
            o_ref[...] = jnp.where(mine, y, o_ref[...])


def _grouped_experts(xs, items, wg, wu, wd):
    r, d = xs.shape
    ne, _, f = wg.shape
    tile = MOE_TILE
    n_items = items[0].shape[0]
    return pl.pallas_call(
        _grouped_kernel,
        grid_spec=pltpu.PrefetchScalarGridSpec(
            num_scalar_prefetch=5,
            grid=(n_items,),
            in_specs=[
                pl.BlockSpec((tile, d), lambda i, tl, ex, lo, hi, fi: (tl[i], 0)),
                pl.BlockSpec((None, d, f), lambda i, tl, ex, lo, hi, fi: (ex[i], 0, 0)),
                pl.BlockSpec((None, d, f), lambda i, tl, ex, lo, hi, fi: (ex[i], 0, 0)),
                pl.BlockSpec((None, f, d), lambda i, tl, ex, lo, hi, fi: (ex[i], 0, 0)),
            ],
            out_specs=pl.BlockSpec((tile, d), lambda i, tl, ex, lo, hi, fi: (tl[i], 0)),
        ),
        out_shape=jax.ShapeDtypeStruct((r, d), F32),
        compiler_params=_params(1),
        name="moe_grouped",
    )(*items, xs, wg, wu, wd)


def _combine_kernel(d1_ref, d2_ref, ys_hbm, h_ref, route_ref, lg_ref, lb_ref, p_ref, wproj_ref, wpg_ref, bpg_ref,
                    o_ref, y_scr, sem):
    tm = h_ref.shape[0]
    base = pl.program_id(0) * tm

    def copies(j):
        t = base + j
        return (_row_copy(ys_hbm, d1_ref[t], y_scr.at[0], j, sem), _row_copy(ys_hbm, d2_ref[t], y_scr.at[1], j, sem))

    def start(j, carry):
        for cp in copies(j):
            cp.start()
        return carry

    def wait(j, carry):
        for cp in copies(j):
            cp.wait()
        return carry

    lax.fori_loop(0, tm, start, 0, unroll=8)
    lax.fori_loop(0, tm, wait, 0, unroll=8)

    route = route_ref[...]
    lane = lax.broadcasted_iota(jnp.int32, route.shape, 1)
    w1 = jnp.sum(jnp.where(lane == ROUTE_W1, route, 0.0), axis=-1, keepdims=True)
    w2 = jnp.sum(jnp.where(lane == ROUTE_W2, route, 0.0), axis=-1, keepdims=True)
    ff = w1 * y_scr[0] + w2 * y_scr[1]
    o_ref[...] = _ffn_epilogue(h_ref[...], ff, lg_ref, lb_ref, p_ref, wproj_ref, wpg_ref, bpg_ref)


def _combine(ys, dest1, dest2, h, route, lg, lb, p3d, layer, wproj, wpg, bpg):
    t, d = h.shape
    dp = p3d.shape[2]
    tm = COMBINE_TM
    row = lambda v: v.reshape(1, -1)
    const = lambda shape: pl.BlockSpec(shape, lambda i, d1, d2: (0,) * len(shape), pipeline_mode=pl.Buffered(1))
    return pl.pallas_call(
        _combine_kernel,
        grid_spec=pltpu.PrefetchScalarGridSpec(
            num_scalar_prefetch=2,
            grid=(t // tm,),
            in_specs=[
                pl.BlockSpec(memory_space=pl.ANY),
                pl.BlockSpec((tm, d), lambda i, d1, d2: (i, 0)),
                pl.BlockSpec((tm, LANES), lambda i, d1, d2: (i, 0)),
                const((1, d)), const((1, d)),
                pl.BlockSpec((None, tm, dp), lambda i, d1, d2: (layer, i, 0)),
                const((dp, d)), const((d, d)), const((1, d)),
            ],
            out_specs=pl.BlockSpec((tm, d), lambda i, d1, d2: (i, 0)),
            scratch_shapes=[pltpu.VMEM((2, tm, d), F32), pltpu.SemaphoreType.DMA(())],
        ),
        out_shape=jax.ShapeDtypeStruct((t, d), F32),
        compiler_params=_params(1),
        name="moe_combine",
    )(dest1, dest2, ys, h, route, row(lg), row(lb), p3d, wproj, wpg, row(bpg))


def _moe(h, route, cnt, wg, wu, wd, lg, lb, p3d, layer, wproj, wpg, bpg):
    dest1, dest2, items = _route_plan(route, cnt, MOE_TILE)
    xs = _dispatch(h, dest1, dest2)
    ys = _grouped_experts(xs, items, wg, wu, wd)
    return _combine(ys, dest1, dest2, h, route, lg, lb, p3d, layer, wproj, wpg, bpg)


def kernel(x, p, conv_w_pw1, conv_b_pw1, conv_w_dw, conv_b_dw, conv_ln_g, conv_ln_b, conv_w_pw2, conv_b_pw2, attn_w_qkv, attn_w_o, ffn_w_gate, ffn_w_up, ffn_w_down, moe_w_router, moe_b_router, moe_w_gate, moe_w_up, moe_w_down, ln_mix_g, ln_mix_b, ln_ffn_g, ln_ffn_b, ple_w_proj, ple_w_gate, ple_b_gate):
    batch, seq, d = x.shape
    t = batch * seq
    bf = lambda w: w.astype(BF16)
    x2d = x.reshape(t, d)
    p3d = p.reshape(p.shape[0], t, p.shape[-1])

    h = _conv_mixer(x2d, batch, seq, bf(conv_w_pw1[0]), conv_b_pw1[0], conv_w_dw[0], conv_b_dw[0],
                    conv_ln_g[0], conv_ln_b[0], bf(conv_w_pw2[0]), conv_b_pw2[0], ln_mix_g[0], ln_mix_b[0])
    h = _ffn0(h, bf(ffn_w_gate[0]), bf(ffn_w_up[0]), bf(ffn_w_down[0]), ln_ffn_g[0], ln_ffn_b[0],
              p3d, 0, bf(ple_w_proj[0]), bf(ple_w_gate[0]), ple_b_gate[0])

    qkv = _qkv(h, bf(attn_w_qkv[0]))
    o = _attention(qkv, batch, seq)
    wr = jnp.zeros((d, LANES), F32).at[:, :N_EXPERTS].set(moe_w_router[0])
    wr_hi = wr.astype(BF16)
    wr_lo = (wr - wr_hi.astype(F32)).astype(BF16)
    br = jnp.full((1, LANES), -1e30, F32).at[0, :N_EXPERTS].set(moe_b_router[0])
    h, route, cnt = _attn_out(o, h, bf(attn_w_o[0]), ln_mix_g[1], ln_mix_b[1], wr_hi, wr_lo, br)
    h = _moe(h, route, cnt, bf(moe_w_gate[0]), bf(moe_w_up[0]), bf(moe_w_down[0]), ln_ffn_g[1], ln_ffn_b[1],
             p3d, 1, bf(ple_w_proj[1]), bf(ple_w_gate[1]), ple_b_gate[1])
    return h.reshape(batch, seq, d)
```

```python
import functools
import math

import jax
import jax.numpy as jnp
from jax import lax
from jax.experimental import pallas as pl
from jax.experimental.pallas import tpu as pltpu

F32 = jnp.float32
BF16 = jnp.bfloat16

D_MODEL = 1024
N_HEADS = 16
HEAD_DIM = D_MODEL // N_HEADS
CONV_WIDTH = 31
N_EXPERTS = 8
LN_EPS = 1e-5
DEPTH = 2
ALPHA = (2.0 * DEPTH) ** 0.25
LOG2E = math.log2(math.e)

LANES = 128
SUBLANES = 8
VMEM_LIMIT = 56 * 1024 * 1024

CONV_TM = 512
CONV_HALO = 32
FFN_TM = 512
QKV_TM = 512
ATT_T = 256
OUT_TM = 512
MOE_TILE = 512
DISPATCH_TB = 512
COMBINE_TM = 512


def _ln(x, g, b):
    mu = jnp.mean(x, axis=-1, keepdims=True)
    xc = x - mu
    var = jnp.mean(xc * xc, axis=-1, keepdims=True)
    return xc * lax.rsqrt(var + LN_EPS) * g + b


def _sigmoid(x):
    return 1.0 / (1.0 + jnp.exp(-x))


def _dot(a, b):
    return jnp.dot(a, b, preferred_element_type=F32)


def _const_spec(shape):
    nd = len(shape)
    return pl.BlockSpec(shape, lambda *_: (0,) * nd, pipeline_mode=pl.Buffered(1))


def _params(n_axes):
    return pltpu.CompilerParams(dimension_semantics=("arbitrary",) * n_axes,
                                vmem_limit_bytes=VMEM_LIMIT)


def _dwconv(g_scr, wb_ref, bdw_ref, c_scr, tm):
    d = g_scr.shape[1]
    off = CONV_HALO - (CONV_WIDTH - 1)
    sub = lax.broadcasted_iota(jnp.int32, (SUBLANES, LANES), 0)
    masks = [sub >= r for r in range(SUBLANES)]
    n_a = (off + CONV_WIDTH - 1) // SUBLANES

    def valid(m):
        return off <= m < off + CONV_WIDTH

    for l in range(d // LANES):
        cols = slice(l * LANES, (l + 1) * LANES)

        def wv(m, cols=cols):
            return wb_ref[m - off, :, cols]

        def tile(j, cols=cols):
            if isinstance(j, int):
                return g_scr[j * SUBLANES:(j + 1) * SUBLANES, cols]
            return g_scr[pl.ds(pl.multiple_of(j * SUBLANES, SUBLANES), SUBLANES), cols]

        def shifted_sums(t, wv=wv):
            out = []
            for r in range(1, SUBLANES):
                acc = None
                for a in range(n_a):
                    m = SUBLANES * a + r
                    if valid(m):
                        term = wv(m) * t[a]
                        acc = term if acc is None else acc + term
                out.append(acc)
            return tuple(out)

        bias = jnp.broadcast_to(bdw_ref[:, cols], (SUBLANES, LANES))

        def body(i, carry, cols=cols, bias=bias, wv=wv, tile=tile, shifted_sums=shifted_sums):
            t = [tile(i + 1 + a) for a in range(n_a)]
            nxt = shifted_sums(t)
            acc = bias
            for a in range(1, n_a + 1):
                if valid(SUBLANES * a):
                    acc = acc + wv(SUBLANES * a) * t[a - 1]
            for r in range(1, SUBLANES):
                y = jnp.where(masks[r], carry[r - 1], nxt[r - 1])
                acc = acc + pltpu.roll(y, SUBLANES - r, 0)
            c_scr[pl.ds(pl.multiple_of(i * SUBLANES, SUBLANES), SUBLANES), cols] = acc
            return nxt

        lax.fori_loop(0, tm // SUBLANES, body, shifted_sums([tile(a) for a in range(n_a)]), unroll=4)


def _conv_mixer_kernel(x_ref, w1_ref, b1_ref, wb_ref, bdw_ref, lng_ref, lnb_ref, w2_ref, b2_ref,
                       mg_ref, mb_ref, o_ref, g_scr, c_scr):
    tm = x_ref.shape[0]
    d = x_ref.shape[1]

    @pl.when(pl.program_id(1) == 0)
    def _():
        g_scr[0:CONV_HALO, :] = jnp.zeros((CONV_HALO, d), F32)

    x = x_ref[...]
    h = _dot(x.astype(BF16), w1_ref[...]) + b1_ref[...]
    g_scr[CONV_HALO:CONV_HALO + tm, :] = h[:, :d] * _sigmoid(h[:, d:])
    _dwconv(g_scr, wb_ref, bdw_ref, c_scr, tm)
    g_scr[0:CONV_HALO, :] = g_scr[tm:tm + CONV_HALO, :]

    hc = _ln(c_scr[...], lng_ref[...], lnb_ref[...])
    hc = hc * _sigmoid(hc)
    mix = _dot(hc.astype(BF16), w2_ref[...]) + b2_ref[...]
    o_ref[...] = _ln(ALPHA * x + mix, mg_ref[...], mb_ref[...])


def _conv_mixer(x2d, batch, seq, w1, b1, wdw, bdw, lng, lnb, w2, b2, mg, mb):
    t, d = x2d.shape
    tm = CONV_TM
    ns = seq // tm
    row = lambda v: v.reshape(1, -1)
    return pl.pallas_call(
        _conv_mixer_kernel,
        grid=(batch, ns),
        in_specs=[
            pl.BlockSpec((tm, d), lambda b, s: (b * ns + s, 0)),
            _const_spec((d, 2 * d)), _const_spec((1, 2 * d)),
            _const_spec((CONV_WIDTH, SUBLANES, d)), _const_spec((1, d)),
            _const_spec((1, d)), _const_spec((1, d)),
            _const_spec((d, d)), _const_spec((1, d)),
            _const_spec((1, d)), _const_spec((1, d)),
        ],
        out_specs=pl.BlockSpec((tm, d), lambda b, s: (b * ns + s, 0)),
        out_shape=jax.ShapeDtypeStruct((t, d), F32),
        scratch_shapes=[pltpu.VMEM((CONV_HALO + tm, d), F32), pltpu.VMEM((tm, d), F32)],
        compiler_params=_params(2),
        name="conv_mixer",
    )(x2d, w1, row(b1), jnp.broadcast_to(wdw[:, None, :], (CONV_WIDTH, SUBLANES, d)), row(bdw),
      row(lng), row(lnb), w2, row(b2), row(mg), row(mb))


def _ffn_epilogue(h, ff, lg_ref, lb_ref, p_ref, wproj_ref, wpg_ref, bpg_ref):
    y = _ln(ALPHA * h + ff, lg_ref[...], lb_ref[...])
    gate = _sigmoid(_dot(y.astype(BF16), wpg_ref[...]) + bpg_ref[...])
    return y + gate * _dot(p_ref[...].astype(BF16), wproj_ref[...])


def _ffn0_kernel(h_ref, wg_ref, wu_ref, wd_ref, lg_ref, lb_ref, p_ref, wproj_ref, wpg_ref, bpg_ref, o_ref):
    h = h_ref[...]
    hb = h.astype(BF16)
    gate = _dot(hb, wg_ref[...])
    up = _dot(hb, wu_ref[...])
    act = (gate * _sigmoid(gate) * up).astype(BF16)
    ff = _dot(act, wd_ref[...])
    o_ref[...] = _ffn_epilogue(h, ff, lg_ref, lb_ref, p_ref, wproj_ref, wpg_ref, bpg_ref)


def _ffn0(h, wg, wu, wd, lg, lb, p3d, layer, wproj, wpg, bpg):
    t, d = h.shape
    f = wg.shape[1]
    dp = p3d.shape[2]
    tm = FFN_TM
    row = lambda v: v.reshape(1, -1)
    return pl.pallas_call(
        _ffn0_kernel,
        grid=(t // tm,),
        in_specs=[
            pl.BlockSpec((tm, d), lambda i: (i, 0)),
            _const_spec((d, f)), _const_spec((d, f)), _const_spec((f, d)),
            _const_spec((1, d)), _const_spec((1, d)),
            pl.BlockSpec((None, tm, dp), lambda i: (layer, i, 0)),
            _const_spec((dp, d)), _const_spec((d, d)), _const_spec((1, d)),
        ],
        out_specs=pl.BlockSpec((tm, d), lambda i: (i, 0)),
        out_shape=jax.ShapeDtypeStruct((t, d), F32),
        compiler_params=_params(1),
        name="ffn0",
    )(h, wg, wu, wd, row(lg), row(lb), p3d, wproj, wpg, row(bpg))


def _qkv_kernel(h_ref, w_ref, o_ref):
    d = h_ref.shape[1]
    qkv = _dot(h_ref[...].astype(BF16), w_ref[...])
    o_ref[:, :d] = (qkv[:, :d] * (HEAD_DIM ** -0.5 * LOG2E)).astype(BF16)
    o_ref[:, d:] = qkv[:, d:].astype(BF16)


def _qkv(h, w):
    t, d = h.shape
    n = w.shape[1]
    tm = QKV_TM
    return pl.pallas_call(
        _qkv_kernel,
        grid=(t // tm,),
        in_specs=[pl.BlockSpec((tm, d), lambda i: (i, 0)), _const_spec((d, n))],
        out_specs=pl.BlockSpec((tm, n), lambda i: (i, 0)),
        out_shape=jax.ShapeDtypeStruct((t, n), BF16),
        compiler_params=_params(1),
        name="qkv_proj",
    )(h, w)


def _attn_kernel(q_ref, k_ref, v_ref, tri_ref, o_ref, qs_scr, acc_scr, r_scr, *stage_scr):
    tq = ATT_T
    n_q = q_ref.shape[0] // tq
    stages = (stage_scr[:2], stage_scr[2:])
    items = [(qi, kb, h) for qi in range(n_q) for kb in range(qi, -1, -1) for h in (0, 1)]

    def rows(h):
        return slice(h * tq, (h + 1) * tq)

    def block(i):
        return slice(i * tq, (i + 1) * tq)

    lane = lax.broadcasted_iota(jnp.int32, (tq, LANES), 1)
    for qi in range(n_q):
        nq = -q_ref[block(qi), :]
        zero = jnp.zeros_like(nq)
        qs_scr[qi, rows(0), :] = jnp.where(lane < HEAD_DIM, nq, zero)
        qs_scr[qi, rows(1), :] = jnp.where(lane >= HEAD_DIM, nq, zero)

    def score_matmul(n):
        qi, kb, h = items[n]
        return lax.dot_general(qs_scr[qi, rows(h), :], k_ref[block(kb), :], (((1,), (1,)), ((), ())),
                               preferred_element_type=F32)

    def score_rest(zn, n):
        qi, kb, h = items[n]
        zn_scr, l_scr = stages[(n // 2) % 2]
        if kb == qi:
            row = lax.broadcasted_iota(jnp.int32, zn.shape, 0)
            col = lax.broadcasted_iota(jnp.int32, zn.shape, 1)
            zn = jnp.where(col < row, zn, jnp.inf)
        log1m = jnp.minimum(zn, 0.0) - jnp.log(1.0 + jnp.exp2(-jnp.abs(zn))) * LOG2E
        zn_scr[rows(h), :] = zn
        l_scr[rows(h), :] = log1m.astype(BF16)

    def cumsum_matmul(n):
        h = items[n][2]
        return _dot(stages[(n // 2) % 2][1][rows(h), :], tri_ref[...])

    def accumulate_rest(csum, n):
        qi, kb, h = items[n]
        zn = stages[(n // 2) % 2][0][rows(h), :]
        if kb == qi:
            w = jnp.exp2(csum - zn)
            carried = csum[:, 0:1]
        else:
            r = r_scr[rows(h), :]
            w = jnp.exp2(csum + r - zn)
            carried = r + csum[:, 0:1]
        acc = _dot(w.astype(BF16), v_ref[block(kb), :])
        if kb != qi:
            acc = acc_scr[rows(h), :] + acc
        if kb != 0:
            r_scr[rows(h), :] = carried
            acc_scr[rows(h), :] = acc
        elif h == 0:
            acc_scr[rows(h), :] = acc
        else:
            o_ref[block(qi), :] = jnp.where(lane < HEAD_DIM, acc_scr[rows(0), :], acc).astype(BF16)

    n_items = len(items)
    score_rest(score_matmul(0), 0)
    score_rest(score_matmul(1), 1)
    csums, scores = {}, {}
    for n in range(n_items):
        csums[n] = cumsum_matmul(n)
        if n + 2 < n_items:
            scores[n + 2] = score_matmul(n + 2)
        if n >= 1:
            accumulate_rest(csums.pop(n - 1), n - 1)
            if n + 1 < n_items:
                score_rest(scores.pop(n + 1), n + 1)
    accumulate_rest(csums.pop(n_items - 1), n_items - 1)


def _attention(qkv, batch, seq):
    t = qkv.shape[0]
    tq = ATT_T
    nq = seq // tq
    hp = D_MODEL // LANES
    j = lax.broadcasted_iota(jnp.int32, (tq, tq), 0)
    s = lax.broadcasted_iota(jnp.int32, (tq, tq), 1)
    tri = (j >= s).astype(BF16)
    return pl.pallas_call(
        _attn_kernel,
        grid=(batch, hp),
        in_specs=[
            pl.BlockSpec((seq, LANES), lambda b, h: (b, h)),
            pl.BlockSpec((seq, LANES), lambda b, h: (b, hp + h)),
            pl.BlockSpec((seq, LANES), lambda b, h: (b, 2 * hp + h)),
            _const_spec((tq, tq)),
        ],
        out_specs=pl.BlockSpec((seq, LANES), lambda b, h: (b, h)),
        out_shape=jax.ShapeDtypeStruct((t, D_MODEL), BF16),
        scratch_shapes=[pltpu.VMEM((nq, 2 * tq, LANES), BF16), pltpu.VMEM((2 * tq, LANES), F32),
                        pltpu.VMEM((2 * tq, 1), F32)]
        + 2 * [pltpu.VMEM((2 * tq, tq), F32), pltpu.VMEM((2 * tq, tq), BF16)],
        compiler_params=_params(2),
        name="stick_attn",
    )(qkv, qkv, qkv, tri)


ROUTE_E1, ROUTE_E2, ROUTE_W1, ROUTE_W2, ROUTE_R1, ROUTE_R2 = range(6)


def _attn_out_kernel(o_ref, h_ref, wo_ref, mg_ref, mb_ref, wrh_ref, wrl_ref, br_ref, ltri_ref,
                     y_ref, route_ref, cnt_ref, cnt_scr):
    @pl.when(pl.program_id(0) == 0)
    def _():
        cnt_scr[...] = jnp.zeros(cnt_scr.shape, F32)

    y = _ln(ALPHA * h_ref[...] + _dot(o_ref[...], wo_ref[...]), mg_ref[...], mb_ref[...])
    y_ref[...] = y
    yh = y.astype(BF16)
    yl = (y - yh.astype(F32)).astype(BF16)
    logits = _dot(yh, wrh_ref[...]) + _dot(yl, wrh_ref[...]) + _dot(yh, wrl_ref[...]) + br_ref[...]
    lane = lax.broadcasted_iota(jnp.int32, logits.shape, 1)
    m1 = jnp.max(logits, axis=-1, keepdims=True)
    i1 = jnp.min(jnp.where(logits == m1, lane, LANES), axis=-1, keepdims=True)
    rest = jnp.where(lane == i1, -jnp.inf, logits)
    m2 = jnp.max(rest, axis=-1, keepdims=True)
    i2 = jnp.min(jnp.where(rest == m2, lane, LANES), axis=-1, keepdims=True)
    e = jnp.exp(m2 - m1)
    w1 = 1.0 / (1.0 + e)
    w2 = e / (1.0 + e)
    oh1 = lane == i1
    oh2 = lane == i2
    sel = jnp.where(oh1 | oh2, 1.0, 0.0)
    before = _dot(ltri_ref[...], sel.astype(BF16)) + cnt_scr[...]
    rank1 = jnp.sum(jnp.where(oh1, before, 0.0), axis=-1, keepdims=True)
    rank2 = jnp.sum(jnp.where(oh2, before, 0.0), axis=-1, keepdims=True)
    cnt = cnt_scr[...] + jnp.sum(sel, axis=0, keepdims=True)
    cnt_scr[...] = cnt
    cnt_ref[...] = cnt
    route = jnp.zeros(logits.shape, F32)
    for c, v in ((ROUTE_E1, i1.astype(F32)), (ROUTE_E2, i2.astype(F32)), (ROUTE_W1, w1), (ROUTE_W2, w2),
                 (ROUTE_R1, rank1), (ROUTE_R2, rank2)):
        route = jnp.where(lane == c, v, route)
    route_ref[...] = route


def _attn_out(o, h, wo, mg, mb, wr_hi, wr_lo, br):
    t, d = h.shape
    tm = OUT_TM
    row = lambda v: v.reshape(1, -1)
    r = lax.broadcasted_iota(jnp.int32, (tm, tm), 0)
    c = lax.broadcasted_iota(jnp.int32, (tm, tm), 1)
    ltri = (c < r).astype(BF16)
    return pl.pallas_call(
        _attn_out_kernel,
        grid=(t // tm,),
        in_specs=[
            pl.BlockSpec((tm, d), lambda i: (i, 0)),
            pl.BlockSpec((tm, d), lambda i: (i, 0)),
            _const_spec((d, d)), _const_spec((1, d)), _const_spec((1, d)),
            _const_spec((d, LANES)), _const_spec((d, LANES)), _const_spec((1, LANES)),
            _const_spec((tm, tm)),
        ],
        out_specs=[pl.BlockSpec((tm, d), lambda i: (i, 0)), pl.BlockSpec((tm, LANES), lambda i: (i, 0)),
                   pl.BlockSpec((1, LANES), lambda i: (0, 0))],
        out_shape=[jax.ShapeDtypeStruct((t, d), F32), jax.ShapeDtypeStruct((t, LANES), F32),
                   jax.ShapeDtypeStruct((1, LANES), F32)],
        scratch_shapes=[pltpu.VMEM((1, LANES), F32)],
        compiler_params=_params(1),
        name="attn_out_router",
    )(o, h, wo, row(mg), row(mb), wr_hi, wr_lo, br, ltri)


def _route_plan(route, cnt, tile):
    t = route.shape[0]
    as_int = lambda lane: route[:, lane].astype(jnp.int32)
    counts = cnt[0, :N_EXPERTS].astype(jnp.int32)
    ends = jnp.cumsum(counts)
    starts = ends - counts
    dest1 = jnp.take(starts, as_int(ROUTE_E1)) + as_int(ROUTE_R1)
    dest2 = jnp.take(starts, as_int(ROUTE_E2)) + as_int(ROUTE_R2)

    n_tiles = 2 * t // tile
    n_items = n_tiles + N_EXPERTS - 1
    tile_lo = jnp.arange(n_tiles, dtype=jnp.int32)[:, None] * tile
    lo = jnp.maximum(tile_lo, starts[None, :])
    hi = jnp.minimum(tile_lo + tile, ends[None, :])
    valid = (hi > lo).reshape(-1)
    n_valid = jnp.sum(valid.astype(jnp.int32))
    (idx,) = jnp.nonzero(valid, size=n_items, fill_value=0)
    idx = idx.astype(jnp.int32)
    is_pad = jnp.arange(n_items, dtype=jnp.int32) >= n_valid
    idx = jnp.where(is_pad, idx[n_valid - 1], idx)
    it_tile = idx // N_EXPERTS
    it_exp = idx % N_EXPERTS
    it_lo = jnp.where(is_pad, 0, lo.reshape(-1)[idx] - it_tile * tile)
    it_hi = jnp.where(is_pad, 0, hi.reshape(-1)[idx] - it_tile * tile)
    it_first = jnp.concatenate([jnp.ones((1,), jnp.int32), (it_tile[1:] != it_tile[:-1]).astype(jnp.int32)])
    return dest1, dest2, (it_tile, it_exp, it_lo, it_hi, it_first)


def _row_copy(src, src_row, dst, dst_row, sem):
    return pltpu.make_async_copy(src.at[pl.ds(src_row, 1), :], dst.at[pl.ds(dst_row, 1), :], sem)


def _dispatch_kernel(d1_ref, d2_ref, h_ref, xs_hbm, sem):
    base = pl.program_id(0) * DISPATCH_TB

    def copies(j):
        t = base + j
        return (_row_copy(h_ref, j, xs_hbm, d1_ref[t], sem), _row_copy(h_ref, j, xs_hbm, d2_ref[t], sem))

    def start(j, carry):
        for cp in copies(j):
            cp.start()
        return carry

    def wait(j, carry):
        for cp in copies(j):
            cp.wait()
        return carry

    lax.fori_loop(0, DISPATCH_TB, start, 0, unroll=8)
    lax.fori_loop(0, DISPATCH_TB, wait, 0, unroll=8)


def _dispatch(h, dest1, dest2):
    t, d = h.shape
    return pl.pallas_call(
        _dispatch_kernel,
        grid_spec=pltpu.PrefetchScalarGridSpec(
            num_scalar_prefetch=2,
            grid=(t // DISPATCH_TB,),
            in_specs=[pl.BlockSpec((DISPATCH_TB, d), lambda i, d1, d2: (i, 0))],
            out_specs=pl.BlockSpec(memory_space=pl.ANY),
            scratch_shapes=[pltpu.SemaphoreType.DMA(())],
        ),
        out_shape=jax.ShapeDtypeStruct((2 * t, d), F32),
        compiler_params=_params(1),
        name="moe_dispatch",
    )(dest1, dest2, h)


def _grouped_kernel(tile_ref, exp_ref, lo_ref, hi_ref, first_ref, x_ref, wg_ref, wu_ref, wd_ref, o_ref):
    i = pl.program_id(0)
    lo = lo_ref[i]
    hi = hi_ref[i]

    @pl.when(hi > lo)
    def _():
        xb = x_ref[...].astype(BF16)
        gate = _dot(xb, wg_ref[...])
        up = _dot(xb, wu_ref[...])
        act = (gate * _sigmoid(gate) * up).astype(BF16)
        y = _dot(act, wd_ref[...])
        row = lax.broadcasted_iota(jnp.int32, y.shape, 0)
        mine = (row >= lo) & (row < hi)

        @pl.when(first_ref[i] == 1)
        def _():
            o_ref[...] = jnp.where(mine, y, 0.0)

        @pl.when(first_ref[i] == 0)
        def _():
            o_ref[...] = jnp.where(mine, y, o_ref[...])


def _grouped_experts(xs, items, wg, wu, wd):
    r, d = xs.shape
    ne, _, f = wg.shape
    tile = MOE_TILE
    n_items = items[0].shape[0]
    return pl.pallas_call(
        _grouped_kernel,
        grid_spec=pltpu.PrefetchScalarGridSpec(
            num_scalar_prefetch=5,
            grid=(n_items,),
            in_specs=[
                pl.BlockSpec((tile, d), lambda i, tl, ex, lo, hi, fi: (tl[i], 0)),
                pl.BlockSpec((None, d, f), lambda i, tl, ex, lo, hi, fi: (ex[i], 0, 0)),
                pl.BlockSpec((None, d, f), lambda i, tl, ex, lo, hi, fi: (ex[i], 0, 0)),
                pl.BlockSpec((None, f, d), lambda i, tl, ex, lo, hi, fi: (ex[i], 0, 0)),
            ],
            out_specs=pl.BlockSpec((tile, d), lambda i, tl, ex, lo, hi, fi: (tl[i], 0)),
        ),
        out_shape=jax.ShapeDtypeStruct((r, d), F32),
        compiler_params=_params(1),
        name="moe_grouped",
    )(*items, xs, wg, wu, wd)


def _combine_kernel(d1_ref, d2_ref, ys_hbm, h_ref, route_ref, lg_ref, lb_ref, p_ref, wproj_ref, wpg_ref, bpg_ref,
                    o_ref, y_scr, sem):
    tm = h_ref.shape[0]
    i = pl.program_id(0)
    last = pl.num_programs(0) - 1
    slot = i % 2

    def copies(tile, s, j):
        t = tile * tm + j
        return (_row_copy(ys_hbm, d1_ref[t], y_scr.at[s, 0], j, sem.at[s]),
                _row_copy(ys_hbm, d2_ref[t], y_scr.at[s, 1], j, sem.at[s]))

    def start_loop(tile, s):
        def body(j, carry):
            for cp in copies(tile, s, j):
                cp.start()
            return carry
        lax.fori_loop(0, tm, body, 0, unroll=8)

    def wait_loop(tile, s):
        def body(j, carry):
            for cp in copies(tile, s, j):
                cp.wait()
            return carry
        lax.fori_loop(0, tm, body, 0, unroll=8)

    @pl.when(i == 0)
    def _():
        start_loop(0, 0)

    wait_loop(i, slot)

    route = route_ref[...]
    lane = lax.broadcasted_iota(jnp.int32, route.shape, 1)
    w1 = jnp.sum(jnp.where(lane == ROUTE_W1, route, 0.0), axis=-1, keepdims=True)
    w2 = jnp.sum(jnp.where(lane == ROUTE_W2, route, 0.0), axis=-1, keepdims=True)
    ff = w1 * y_scr[slot, 0] + w2 * y_scr[slot, 1]
    o_ref[...] = _ffn_epilogue(h_ref[...], ff, lg_ref, lb_ref, p_ref, wproj_ref, wpg_ref, bpg_ref)

    nxt = jnp.minimum(i + 1, last)
    for j in range(tm):
        for cp in copies(nxt, 1 - slot, j):
            cp.start()

    @pl.when(i == last)
    def _():
        wait_loop(last, 1 - slot)


def _combine(ys, dest1, dest2, h, route, lg, lb, p3d, layer, wproj, wpg, bpg):
    t, d = h.shape
    dp = p3d.shape[2]
    tm = COMBINE_TM
    row = lambda v: v.reshape(1, -1)
    const = lambda shape: pl.BlockSpec(shape, lambda i, d1, d2: (0,) * len(shape), pipeline_mode=pl.Buffered(1))
    return pl.pallas_call(
        _combine_kernel,
        grid_spec=pltpu.PrefetchScalarGridSpec(
            num_scalar_prefetch=2,
            grid=(t // tm,),
            in_specs=[
                pl.BlockSpec(memory_space=pl.ANY),
                pl.BlockSpec((tm, d), lambda i, d1, d2: (i, 0)),
                pl.BlockSpec((tm, LANES), lambda i, d1, d2: (i, 0)),
                const((1, d)), const((1, d)),
                pl.BlockSpec((None, tm, dp), lambda i, d1, d2: (layer, i, 0)),
                const((dp, d)), const((d, d)), const((1, d)),
            ],
            out_specs=pl.BlockSpec((tm, d), lambda i, d1, d2: (i, 0)),
            scratch_shapes=[pltpu.VMEM((2, 2, tm, d), F32), pltpu.SemaphoreType.DMA((2,))],
        ),
        out_shape=jax.ShapeDtypeStruct((t, d), F32),
        compiler_params=_params(1),
        name="moe_combine",
    )(dest1, dest2, ys, h, route, row(lg), row(lb), p3d, wproj, wpg, row(bpg))


def _moe(h, route, cnt, wg, wu, wd, lg, lb, p3d, layer, wproj, wpg, bpg):
    dest1, dest2, items = _route_plan(route, cnt, MOE_TILE)
    xs = _dispatch(h, dest1, dest2)
    ys = _grouped_experts(xs, items, wg, wu, wd)
    return _combine(ys, dest1, dest2, h, route, lg, lb, p3d, layer, wproj, wpg, bpg)


def kernel(x, p, conv_w_pw1, conv_b_pw1, conv_w_dw, conv_b_dw, conv_ln_g, conv_ln_b, conv_w_pw2, conv_b_pw2, attn_w_qkv, attn_w_o, ffn_w_gate, ffn_w_up, ffn_w_down, moe_w_router, moe_b_router, moe_w_gate, moe_w_up, moe_w_down, ln_mix_g, ln_mix_b, ln_ffn_g, ln_ffn_b, ple_w_proj, ple_w_gate, ple_b_gate):
    batch, seq, d = x.shape
    t = batch * seq
    bf = lambda w: w.astype(BF16)
    x2d = x.reshape(t, d)
    p3d = p.reshape(p.shape[0], t, p.shape[-1])

    h = _conv_mixer(x2d, batch, seq, bf(conv_w_pw1[0]), conv_b_pw1[0], conv_w_dw[0], conv_b_dw[0],
                    conv_ln_g[0], conv_ln_b[0], bf(conv_w_pw2[0]), conv_b_pw2[0], ln_mix_g[0], ln_mix_b[0])
    h = _ffn0(h, bf(ffn_w_gate[0]), bf(ffn_w_up[0]), bf(ffn_w_down[0]), ln_ffn_g[0], ln_ffn_b[0],
              p3d, 0, bf(ple_w_proj[0]), bf(ple_w_gate[0]), ple_b_gate[0])

    qkv = _qkv(h, bf(attn_w_qkv[0]))
    o = _attention(qkv, batch, seq)
    wr = jnp.zeros((d, LANES), F32).at[:, :N_EXPERTS].set(moe_w_router[0])
    wr_hi = wr.astype(BF16)
    wr_lo = (wr - wr_hi.astype(F32)).astype(BF16)
    br = jnp.full((1, LANES), -1e30, F32).at[0, :N_EXPERTS].set(moe_b_router[0])
    h, route, cnt = _attn_out(o, h, bf(attn_w_o[0]), ln_mix_g[1], ln_mix_b[1], wr_hi, wr_lo, br)
    h = _moe(h, route, cnt, bf(moe_w_gate[0]), bf(moe_w_up[0]), bf(moe_w_down[0]), ln_ffn_g[1], ln_ffn_b[1],
             p3d, 1, bf(ple_w_proj[1]), bf(ple_w_gate[1]), ple_b_gate[1])
    return h.reshape(batch, seq, d)
```

```python
import functools
import math

import jax
import jax.numpy as jnp
from jax import lax
from jax.experimental import pallas as pl
from jax.experimental.pallas import tpu as pltpu

F32 = jnp.float32
BF16 = jnp.bfloat16

D_MODEL = 1024
N_HEADS = 16
HEAD_DIM = D_MODEL // N_HEADS
CONV_WIDTH = 31
N_EXPERTS = 8
LN_EPS = 1e-5
DEPTH = 2
ALPHA = (2.0 * DEPTH) ** 0.25
LOG2E = math.log2(math.e)

LANES = 128
SUBLANES = 8
VMEM_LIMIT = 56 * 1024 * 1024

CONV_TM = 512
CONV_HALO = 32
FFN_TM = 512
QKV_TM = 512
ATT_T = 256
OUT_TM = 512
MOE_TILE = 512
DISPATCH_TB = 2048
COMBINE_TM = 512


def _ln(x, g, b):
    mu = jnp.mean(x, axis=-1, keepdims=True)
    xc = x - mu
    var = jnp.mean(xc * xc, axis=-1, keepdims=True)
    return xc * lax.rsqrt(var + LN_EPS) * g + b


def _sigmoid(x):
    return 1.0 / (1.0 + jnp.exp(-x))


def _dot(a, b):
    return jnp.dot(a, b, preferred_element_type=F32)


def _const_spec(shape):
    nd = len(shape)
    return pl.BlockSpec(shape, lambda *_: (0,) * nd, pipeline_mode=pl.Buffered(1))


def _params(n_axes):
    return pltpu.CompilerParams(dimension_semantics=("arbitrary",) * n_axes,
                                vmem_limit_bytes=VMEM_LIMIT)


def _dwconv(g_scr, wb_ref, bdw_ref, c_scr, tm):
    d = g_scr.shape[1]
    off = CONV_HALO - (CONV_WIDTH - 1)
    sub = lax.broadcasted_iota(jnp.int32, (SUBLANES, LANES), 0)
    masks = [sub >= r for r in range(SUBLANES)]
    n_a = (off + CONV_WIDTH - 1) // SUBLANES

    def valid(m):
        return off <= m < off + CONV_WIDTH

    for l in range(d // LANES):
        cols = slice(l * LANES, (l + 1) * LANES)

        def wv(m, cols=cols):
            return wb_ref[m - off, :, cols]

        def tile(j, cols=cols):
            if isinstance(j, int):
                return g_scr[j * SUBLANES:(j + 1) * SUBLANES, cols]
            return g_scr[pl.ds(pl.multiple_of(j * SUBLANES, SUBLANES), SUBLANES), cols]

        def shifted_sums(t, wv=wv):
            out = []
            for r in range(1, SUBLANES):
                acc = None
                for a in range(n_a):
                    m = SUBLANES * a + r
                    if valid(m):
                        term = wv(m) * t[a]
                        acc = term if acc is None else acc + term
                out.append(acc)
            return tuple(out)

        bias = jnp.broadcast_to(bdw_ref[:, cols], (SUBLANES, LANES))

        def body(i, carry, cols=cols, bias=bias, wv=wv, tile=tile, shifted_sums=shifted_sums):
            t = [tile(i + 1 + a) for a in range(n_a)]
            nxt = shifted_sums(t)
            acc = bias
            for a in range(1, n_a + 1):
                if valid(SUBLANES * a):
                    acc = acc + wv(SUBLANES * a) * t[a - 1]
            for r in range(1, SUBLANES):
                y = jnp.where(masks[r], carry[r - 1], nxt[r - 1])
                acc = acc + pltpu.roll(y, SUBLANES - r, 0)
            c_scr[pl.ds(pl.multiple_of(i * SUBLANES, SUBLANES), SUBLANES), cols] = acc
            return nxt

        lax.fori_loop(0, tm // SUBLANES, body, shifted_sums([tile(a) for a in range(n_a)]), unroll=8)


def _conv_mixer_kernel(x_ref, w1_ref, b1_ref, wb_ref, bdw_ref, lng_ref, lnb_ref, w2_ref, b2_ref,
                       mg_ref, mb_ref, o_ref, g_scr, c_scr):
    tm = x_ref.shape[0]
    d = x_ref.shape[1]

    @pl.when(pl.program_id(1) == 0)
    def _():
        g_scr[0:CONV_HALO, :] = jnp.zeros((CONV_HALO, d), F32)

    x = x_ref[...]
    h = _dot(x.astype(BF16), w1_ref[...]) + b1_ref[...]
    g_scr[CONV_HALO:CONV_HALO + tm, :] = h[:, :d] * _sigmoid(h[:, d:])
    _dwconv(g_scr, wb_ref, bdw_ref, c_scr, tm)
    g_scr[0:CONV_HALO, :] = g_scr[tm:tm + CONV_HALO, :]

    hc = _ln(c_scr[...], lng_ref[...], lnb_ref[...])
    hc = hc * _sigmoid(hc)
    mix = _dot(hc.astype(BF16), w2_ref[...]) + b2_ref[...]
    o_ref[...] = _ln(ALPHA * x + mix, mg_ref[...], mb_ref[...])


def _conv_mixer(x2d, batch, seq, w1, b1, wdw, bdw, lng, lnb, w2, b2, mg, mb):
    t, d = x2d.shape
    tm = CONV_TM
    ns = seq // tm
    row = lambda v: v.reshape(1, -1)
    return pl.pallas_call(
        _conv_mixer_kernel,
        grid=(batch, ns),
        in_specs=[
            pl.BlockSpec((tm, d), lambda b, s: (b * ns + s, 0)),
            _const_spec((d, 2 * d)), _const_spec((1, 2 * d)),
            _const_spec((CONV_WIDTH, SUBLANES, d)), _const_spec((1, d)),
            _const_spec((1, d)), _const_spec((1, d)),
            _const_spec((d, d)), _const_spec((1, d)),
            _const_spec((1, d)), _const_spec((1, d)),
        ],
        out_specs=pl.BlockSpec((tm, d), lambda b, s: (b * ns + s, 0)),
        out_shape=jax.ShapeDtypeStruct((t, d), F32),
        scratch_shapes=[pltpu.VMEM((CONV_HALO + tm, d), F32), pltpu.VMEM((tm, d), F32)],
        compiler_params=_params(2),
        name="conv_mixer",
    )(x2d, w1, row(b1), jnp.broadcast_to(wdw[:, None, :], (CONV_WIDTH, SUBLANES, d)), row(bdw),
      row(lng), row(lnb), w2, row(b2), row(mg), row(mb))


def _ffn_epilogue(h, ff, lg_ref, lb_ref, p_ref, wproj_ref, wpg_ref, bpg_ref):
    y = _ln(ALPHA * h + ff, lg_ref[...], lb_ref[...])
    gate = _sigmoid(_dot(y.astype(BF16), wpg_ref[...]) + bpg_ref[...])
    return y + gate * _dot(p_ref[...].astype(BF16), wproj_ref[...])


def _ffn0_kernel(h_ref, wg_ref, wu_ref, wd_ref, lg_ref, lb_ref, p_ref, wproj_ref, wpg_ref, bpg_ref, o_ref):
    h = h_ref[...]
    hb = h.astype(BF16)
    gate = _dot(hb, wg_ref[...])
    up = _dot(hb, wu_ref[...])
    act = (gate * _sigmoid(gate) * up).astype(BF16)
    ff = _dot(act, wd_ref[...])
    o_ref[...] = _ffn_epilogue(h, ff, lg_ref, lb_ref, p_ref, wproj_ref, wpg_ref, bpg_ref)


def _ffn0(h, wg, wu, wd, lg, lb, p3d, layer, wproj, wpg, bpg):
    t, d = h.shape
    f = wg.shape[1]
    dp = p3d.shape[2]
    tm = FFN_TM
    row = lambda v: v.reshape(1, -1)
    return pl.pallas_call(
        _ffn0_kernel,
        grid=(t // tm,),
        in_specs=[
            pl.BlockSpec((tm, d), lambda i: (i, 0)),
            _const_spec((d, f)), _const_spec((d, f)), _const_spec((f, d)),
            _const_spec((1, d)), _const_spec((1, d)),
            pl.BlockSpec((None, tm, dp), lambda i: (layer, i, 0)),
            _const_spec((dp, d)), _const_spec((d, d)), _const_spec((1, d)),
        ],
        out_specs=pl.BlockSpec((tm, d), lambda i: (i, 0)),
        out_shape=jax.ShapeDtypeStruct((t, d), F32),
        compiler_params=_params(1),
        name="ffn0",
    )(h, wg, wu, wd, row(lg), row(lb), p3d, wproj, wpg, row(bpg))


def _qkv_kernel(h_ref, w_ref, o_ref):
    d = h_ref.shape[1]
    qkv = _dot(h_ref[...].astype(BF16), w_ref[...])
    o_ref[:, :d] = (qkv[:, :d] * (HEAD_DIM ** -0.5 * LOG2E)).astype(BF16)
    o_ref[:, d:] = qkv[:, d:].astype(BF16)


def _qkv(h, w):
    t, d = h.shape
    n = w.shape[1]
    tm = QKV_TM
    return pl.pallas_call(
        _qkv_kernel,
        grid=(t // tm,),
        in_specs=[pl.BlockSpec((tm, d), lambda i: (i, 0)), _const_spec((d, n))],
        out_specs=pl.BlockSpec((tm, n), lambda i: (i, 0)),
        out_shape=jax.ShapeDtypeStruct((t, n), BF16),
        compiler_params=_params(1),
        name="qkv_proj",
    )(h, w)


def _attn_kernel(q_ref, k_ref, v_ref, tri_ref, o_ref, qs_scr, acc_scr, r_scr, *stage_scr):
    tq = ATT_T
    n_q = q_ref.shape[0] // tq
    stages = (stage_scr[:2], stage_scr[2:])
    items = [(qi, kb, h) for qi in range(n_q) for kb in range(qi, -1, -1) for h in (0, 1)]

    def rows(h):
        return slice(h * tq, (h + 1) * tq)

    def block(i):
        return slice(i * tq, (i + 1) * tq)

    lane = lax.broadcasted_iota(jnp.int32, (tq, LANES), 1)
    for qi in range(n_q):
        nq = -q_ref[block(qi), :]
        zero = jnp.zeros_like(nq)
        qs_scr[qi, rows(0), :] = jnp.where(lane < HEAD_DIM, nq, zero)
        qs_scr[qi, rows(1), :] = jnp.where(lane >= HEAD_DIM, nq, zero)

    def score_matmul(n):
        qi, kb, h = items[n]
        return lax.dot_general(qs_scr[qi, rows(h), :], k_ref[block(kb), :], (((1,), (1,)), ((), ())),
                               preferred_element_type=F32)

    def score_rest(zn, n):
        qi, kb, h = items[n]
        zn_scr, l_scr = stages[(n // 2) % 2]
        if kb == qi:
            row = lax.broadcasted_iota(jnp.int32, zn.shape, 0)
            col = lax.broadcasted_iota(jnp.int32, zn.shape, 1)
            zn = jnp.where(col < row, zn, jnp.inf)
        log1m = jnp.minimum(zn, 0.0) - jnp.log(1.0 + jnp.exp2(-jnp.abs(zn))) * LOG2E
        zn_scr[rows(h), :] = zn
        l_scr[rows(h), :] = log1m.astype(BF16)

    def cumsum_matmul(n):
        h = items[n][2]
        return _dot(stages[(n // 2) % 2][1][rows(h), :], tri_ref[...])

    def accumulate_rest(csum, n):
        qi, kb, h = items[n]
        zn = stages[(n // 2) % 2][0][rows(h), :]
        if kb == qi:
            w = jnp.exp2(csum - zn)
            carried = csum[:, 0:1]
        else:
            r = r_scr[rows(h), :]
            w = jnp.exp2(csum + r - zn)
            carried = r + csum[:, 0:1]
        acc = _dot(w.astype(BF16), v_ref[block(kb), :])
        if kb != qi:
            acc = acc_scr[rows(h), :] + acc
        if kb != 0:
            r_scr[rows(h), :] = carried
            acc_scr[rows(h), :] = acc
        elif h == 0:
            acc_scr[rows(h), :] = acc
        else:
            o_ref[block(qi), :] = jnp.where(lane < HEAD_DIM, acc_scr[rows(0), :], acc).astype(BF16)

    n_items = len(items)
    score_rest(score_matmul(0), 0)
    score_rest(score_matmul(1), 1)
    csums, scores = {}, {}
    for n in range(n_items):
        csums[n] = cumsum_matmul(n)
        if n + 2 < n_items:
            scores[n + 2] = score_matmul(n + 2)
        if n >= 1:
            accumulate_rest(csums.pop(n - 1), n - 1)
            if n + 1 < n_items:
                score_rest(scores.pop(n + 1), n + 1)
    accumulate_rest(csums.pop(n_items - 1), n_items - 1)


def _attention(qkv, batch, seq):
    t = qkv.shape[0]
    tq = ATT_T
    nq = seq // tq
    hp = D_MODEL // LANES
    j = lax.broadcasted_iota(jnp.int32, (tq, tq), 0)
    s = lax.broadcasted_iota(jnp.int32, (tq, tq), 1)
    tri = (j >= s).astype(BF16)
    return pl.pallas_call(
        _attn_kernel,
        grid=(batch, hp),
        in_specs=[
            pl.BlockSpec((seq, LANES), lambda b, h: (b, h)),
            pl.BlockSpec((seq, LANES), lambda b, h: (b, hp + h)),
            pl.BlockSpec((seq, LANES), lambda b, h: (b, 2 * hp + h)),
            _const_spec((tq, tq)),
        ],
        out_specs=pl.BlockSpec((seq, LANES), lambda b, h: (b, h)),
        out_shape=jax.ShapeDtypeStruct((t, D_MODEL), BF16),
        scratch_shapes=[pltpu.VMEM((nq, 2 * tq, LANES), BF16), pltpu.VMEM((2 * tq, LANES), F32),
                        pltpu.VMEM((2 * tq, 1), F32)]
        + 2 * [pltpu.VMEM((2 * tq, tq), F32), pltpu.VMEM((2 * tq, tq), BF16)],
        compiler_params=_params(2),
        name="stick_attn",
    )(qkv, qkv, qkv, tri)


ROUTE_E1, ROUTE_E2, ROUTE_W1, ROUTE_W2, ROUTE_R1, ROUTE_R2 = range(6)


def _attn_out_kernel(o_ref, h_ref, wo_ref, mg_ref, mb_ref, wrh_ref, wrl_ref, br_ref, ltri_ref,
                     y_ref, route_ref, cnt_ref, cnt_scr):
    @pl.when(pl.program_id(0) == 0)
    def _():
        cnt_scr[...] = jnp.zeros(cnt_scr.shape, F32)

    y = _ln(ALPHA * h_ref[...] + _dot(o_ref[...], wo_ref[...]), mg_ref[...], mb_ref[...])
    y_ref[...] = y
    yh = y.astype(BF16)
    yl = (y - yh.astype(F32)).astype(BF16)
    logits = _dot(yh, wrh_ref[...]) + _dot(yl, wrh_ref[...]) + _dot(yh, wrl_ref[...]) + br_ref[...]
    lane = lax.broadcasted_iota(jnp.int32, logits.shape, 1)
    m1 = jnp.max(logits, axis=-1, keepdims=True)
    i1 = jnp.min(jnp.where(logits == m1, lane, LANES), axis=-1, keepdims=True)
    rest = jnp.where(lane == i1, -jnp.inf, logits)
    m2 = jnp.max(rest, axis=-1, keepdims=True)
    i2 = jnp.min(jnp.where(rest == m2, lane, LANES), axis=-1, keepdims=True)
    e = jnp.exp(m2 - m1)
    w1 = 1.0 / (1.0 + e)
    w2 = e / (1.0 + e)
    oh1 = lane == i1
    oh2 = lane == i2
    sel = jnp.where(oh1 | oh2, 1.0, 0.0)
    before = _dot(ltri_ref[...], sel.astype(BF16)) + cnt_scr[...]
    rank1 = jnp.sum(jnp.where(oh1, before, 0.0), axis=-1, keepdims=True)
    rank2 = jnp.sum(jnp.where(oh2, before, 0.0), axis=-1, keepdims=True)
    cnt = cnt_scr[...] + jnp.sum(sel, axis=0, keepdims=True)
    cnt_scr[...] = cnt
    cnt_ref[...] = cnt
    route = jnp.zeros(logits.shape, F32)
    for c, v in ((ROUTE_E1, i1.astype(F32)), (ROUTE_E2, i2.astype(F32)), (ROUTE_W1, w1), (ROUTE_W2, w2),
                 (ROUTE_R1, rank1), (ROUTE_R2, rank2)):
        route = jnp.where(lane == c, v, route)
    route_ref[...] = route


def _attn_out(o, h, wo, mg, mb, wr_hi, wr_lo, br):
    t, d = h.shape
    tm = OUT_TM
    row = lambda v: v.reshape(1, -1)
    r = lax.broadcasted_iota(jnp.int32, (tm, tm), 0)
    c = lax.broadcasted_iota(jnp.int32, (tm, tm), 1)
    ltri = (c < r).astype(BF16)
    return pl.pallas_call(
        _attn_out_kernel,
        grid=(t // tm,),
        in_specs=[
            pl.BlockSpec((tm, d), lambda i: (i, 0)),
            pl.BlockSpec((tm, d), lambda i: (i, 0)),
            _const_spec((d, d)), _const_spec((1, d)), _const_spec((1, d)),
            _const_spec((d, LANES)), _const_spec((d, LANES)), _const_spec((1, LANES)),
            _const_spec((tm, tm)),
        ],
        out_specs=[pl.BlockSpec((tm, d), lambda i: (i, 0)), pl.BlockSpec((tm, LANES), lambda i: (i, 0)),
                   pl.BlockSpec((1, LANES), lambda i: (0, 0))],
        out_shape=[jax.ShapeDtypeStruct((t, d), F32), jax.ShapeDtypeStruct((t, LANES), F32),
                   jax.ShapeDtypeStruct((1, LANES), F32)],
        scratch_shapes=[pltpu.VMEM((1, LANES), F32)],
        compiler_params=_params(1),
        name="attn_out_router",
    )(o, h, wo, row(mg), row(mb), wr_hi, wr_lo, br, ltri)


def _route_plan(route, cnt, tile):
    t = route.shape[0]
    as_int = lambda lane: route[:, lane].astype(jnp.int32)
    counts = cnt[0, :N_EXPERTS].astype(jnp.int32)
    ends = jnp.cumsum(counts)
    starts = ends - counts
    dest1 = jnp.take(starts, as_int(ROUTE_E1)) + as_int(ROUTE_R1)
    dest2 = jnp.take(starts, as_int(ROUTE_E2)) + as_int(ROUTE_R2)

    n_tiles = 2 * t // tile
    n_items = n_tiles + N_EXPERTS - 1
    tile_lo = jnp.arange(n_tiles, dtype=jnp.int32)[:, None] * tile
    lo = jnp.maximum(tile_lo, starts[None, :])
    hi = jnp.minimum(tile_lo + tile, ends[None, :])
    valid = (hi > lo).reshape(-1)
    n_valid = jnp.sum(valid.astype(jnp.int32))
    (idx,) = jnp.nonzero(valid, size=n_items, fill_value=0)
    idx = idx.astype(jnp.int32)
    is_pad = jnp.arange(n_items, dtype=jnp.int32) >= n_valid
    idx = jnp.where(is_pad, idx[n_valid - 1], idx)
    it_tile = idx // N_EXPERTS
    it_exp = idx % N_EXPERTS
    it_lo = jnp.where(is_pad, 0, lo.reshape(-1)[idx] - it_tile * tile)
    it_hi = jnp.where(is_pad, 0, hi.reshape(-1)[idx] - it_tile * tile)
    it_first = jnp.concatenate([jnp.ones((1,), jnp.int32), (it_tile[1:] != it_tile[:-1]).astype(jnp.int32)])
    return dest1, dest2, (it_tile, it_exp, it_lo, it_hi, it_first)


def _row_copy(src, src_row, dst, dst_row, sem):
    return pltpu.make_async_copy(src.at[pl.ds(src_row, 1), :], dst.at[pl.ds(dst_row, 1), :], sem)


def _dispatch_kernel(d1_ref, d2_ref, h_ref, xs_hbm, sem):
    base = pl.program_id(0) * DISPATCH_TB

    def copies(j):
        t = base + j
        return (_row_copy(h_ref, j, xs_hbm, d1_ref[t], sem), _row_copy(h_ref, j, xs_hbm, d2_ref[t], sem))

    def start(j, carry):
        for cp in copies(j):
            cp.start()
        return carry

    def wait(j, carry):
        for cp in copies(j):
            cp.wait()
        return carry

    lax.fori_loop(0, DISPATCH_TB, start, 0, unroll=8)
    lax.fori_loop(0, DISPATCH_TB, wait, 0, unroll=8)


def _dispatch(h, dest1, dest2):
    t, d = h.shape
    return pl.pallas_call(
        _dispatch_kernel,
        grid_spec=pltpu.PrefetchScalarGridSpec(
            num_scalar_prefetch=2,
            grid=(t // DISPATCH_TB,),
            in_specs=[pl.BlockSpec((DISPATCH_TB, d), lambda i, d1, d2: (i, 0))],
            out_specs=pl.BlockSpec(memory_space=pl.ANY),
            scratch_shapes=[pltpu.SemaphoreType.DMA(())],
        ),
        out_shape=jax.ShapeDtypeStruct((2 * t, d), F32),
        compiler_params=_params(1),
        name="moe_dispatch",
    )(dest1, dest2, h)


def _grouped_kernel(tile_ref, exp_ref, lo_ref, hi_ref, first_ref, x_ref, wg_ref, wu_ref, wd_ref, o_ref):
    i = pl.program_id(0)
    lo = lo_ref[i]
    hi = hi_ref[i]

    @pl.when(hi > lo)
    def _():
        xb = x_ref[...].astype(BF16)
        gate = _dot(xb, wg_ref[...])
        up = _dot(xb, wu_ref[...])
        act = (gate * _sigmoid(gate) * up).astype(BF16)
        y = _dot(act, wd_ref[...])
        row = lax.broadcasted_iota(jnp.int32, y.shape, 0)
        mine = (row >= lo) & (row < hi)

        @pl.when(first_ref[i] == 1)
        def _():
            o_ref[...] = jnp.where(mine, y, 0.0)

        @pl.when(first_ref[i] == 0)
        def _():
            o_ref[...] = jnp.where(mine, y, o_ref[...])


def _grouped_experts(xs, items, wg, wu, wd):
    r, d = xs.shape
    ne, _, f = wg.shape
    tile = MOE_TILE
    n_items = items[0].shape[0]
    return pl.pallas_call(
        _grouped_kernel,
        grid_spec=pltpu.PrefetchScalarGridSpec(
            num_scalar_prefetch=5,
            grid=(n_items,),
            in_specs=[
                pl.BlockSpec((tile, d), lambda i, tl, ex, lo, hi, fi: (tl[i], 0)),
                pl.BlockSpec((None, d, f), lambda i, tl, ex, lo, hi, fi: (ex[i], 0, 0)),
                pl.BlockSpec((None, d, f), lambda i, tl, ex, lo, hi, fi: (ex[i], 0, 0)),
                pl.BlockSpec((None, f, d), lambda i, tl, ex, lo, hi, fi: (ex[i], 0, 0)),
            ],
            out_specs=pl.BlockSpec((tile, d), lambda i, tl, ex, lo, hi, fi: (tl[i], 0)),
        ),
        out_shape=jax.ShapeDtypeStruct((r, d), F32),
        compiler_params=_params(1),
        name="moe_grouped",
    )(*items, xs, wg, wu, wd)


def _combine_kernel(d1_ref, d2_ref, ys_hbm, h_ref, route_ref, lg_ref, lb_ref, p_ref, wproj_ref, wpg_ref, bpg_ref,
                    o_ref, y_scr, sem):
    tm = h_ref.shape[0]
    i = pl.program_id(0)
    last = pl.num_programs(0) - 1
    slot = i % 2

    def copies(tile, s, j):
        t = tile * tm + j
        return (_row_copy(ys_hbm, d1_ref[t], y_scr.at[s, 0], j, sem.at[s]),
                _row_copy(ys_hbm, d2_ref[t], y_scr.at[s, 1], j, sem.at[s]))

    def start_loop(tile, s):
        def body(j, carry):
            for cp in copies(tile, s, j):
                cp.start()
            return carry
        lax.fori_loop(0, tm, body, 0, unroll=8)

    def wait_loop(tile, s):
        def body(j, carry):
            for cp in copies(tile, s, j):
                cp.wait()
            return carry
        lax.fori_loop(0, tm, body, 0, unroll=8)

    @pl.when(i == 0)
    def _():
        start_loop(0, 0)

    wait_loop(i, slot)

    route = route_ref[...]
    lane = lax.broadcasted_iota(jnp.int32, route.shape, 1)
    w1 = jnp.sum(jnp.where(lane == ROUTE_W1, route, 0.0), axis=-1, keepdims=True)
    w2 = jnp.sum(jnp.where(lane == ROUTE_W2, route, 0.0), axis=-1, keepdims=True)
    ff = w1 * y_scr[slot, 0] + w2 * y_scr[slot, 1]
    o_ref[...] = _ffn_epilogue(h_ref[...], ff, lg_ref, lb_ref, p_ref, wproj_ref, wpg_ref, bpg_ref)

    nxt = jnp.minimum(i + 1, last)
    for j in range(tm):
        for cp in copies(nxt, 1 - slot, j):
            cp.start()

    @pl.when(i == last)
    def _():
        wait_loop(last, 1 - slot)


def _combine(ys, dest1, dest2, h, route, lg, lb, p3d, layer, wproj, wpg, bpg):
    t, d = h.shape
    dp = p3d.shape[2]
    tm = COMBINE_TM
    row = lambda v: v.reshape(1, -1)
    const = lambda shape: pl.BlockSpec(shape, lambda i, d1, d2: (0,) * len(shape), pipeline_mode=pl.Buffered(1))
    return pl.pallas_call(
        _combine_kernel,
        grid_spec=pltpu.PrefetchScalarGridSpec(
            num_scalar_prefetch=2,
            grid=(t // tm,),
            in_specs=[
                pl.BlockSpec(memory_space=pl.ANY),
                pl.BlockSpec((tm, d), lambda i, d1, d2: (i, 0)),
                pl.BlockSpec((tm, LANES), lambda i, d1, d2: (i, 0)),
                const((1, d)), const((1, d)),
                pl.BlockSpec((None, tm, dp), lambda i, d1, d2: (layer, i, 0)),
                const((dp, d)), const((d, d)), const((1, d)),
            ],
            out_specs=pl.BlockSpec((tm, d), lambda i, d1, d2: (i, 0)),
            scratch_shapes=[pltpu.VMEM((2, 2, tm, d), F32), pltpu.SemaphoreType.DMA((2,))],
        ),
        out_shape=jax.ShapeDtypeStruct((t, d), F32),
        compiler_params=_params(1),
        name="moe_combine",
    )(dest1, dest2, ys, h, route, row(lg), row(lb), p3d, wproj, wpg, row(bpg))


def _moe(h, route, cnt, wg, wu, wd, lg, lb, p3d, layer, wproj, wpg, bpg):
    dest1, dest2, items = _route_plan(route, cnt, MOE_TILE)
    xs = _dispatch(h, dest1, dest2)
    ys = _grouped_experts(xs, items, wg, wu, wd)
    return _combine(ys, dest1, dest2, h, route, lg, lb, p3d, layer, wproj, wpg, bpg)


def kernel(x, p, conv_w_pw1, conv_b_pw1, conv_w_dw, conv_b_dw, conv_ln_g, conv_ln_b, conv_w_pw2, conv_b_pw2, attn_w_qkv, attn_w_o, ffn_w_gate, ffn_w_up, ffn_w_down, moe_w_router, moe_b_router, moe_w_gate, moe_w_up, moe_w_down, ln_mix_g, ln_mix_b, ln_ffn_g, ln_ffn_b, ple_w_proj, ple_w_gate, ple_b_gate):
    batch, seq, d = x.shape
    t = batch * seq
    bf = lambda w: w.astype(BF16)
    x2d = x.reshape(t, d)
    p3d = p.reshape(p.shape[0], t, p.shape[-1])

    h = _conv_mixer(x2d, batch, seq, bf(conv_w_pw1[0]), conv_b_pw1[0], conv_w_dw[0], conv_b_dw[0],
                    conv_ln_g[0], conv_ln_b[0], bf(conv_w_pw2[0]), conv_b_pw2[0], ln_mix_g[0], ln_mix_b[0])
    h = _ffn0(h, bf(ffn_w_gate[0]), bf(ffn_w_up[0]), bf(ffn_w_down[0]), ln_ffn_g[0], ln_ffn_b[0],
              p3d, 0, bf(ple_w_proj[0]), bf(ple_w_gate[0]), ple_b_gate[0])

    qkv = _qkv(h, bf(attn_w_qkv[0]))
    o = _attention(qkv, batch, seq)
    wr = jnp.zeros((d, LANES), F32).at[:, :N_EXPERTS].set(moe_w_router[0])
    wr_hi = wr.astype(BF16)
    wr_lo = (wr - wr_hi.astype(F32)).astype(BF16)
    br = jnp.full((1, LANES), -1e30, F32).at[0, :N_EXPERTS].set(moe_b_router[0])
    h, route, cnt = _attn_out(o, h, bf(attn_w_o[0]), ln_mix_g[1], ln_mix_b[1], wr_hi, wr_lo, br)
    h = _moe(h, route, cnt, bf(moe_w_gate[0]), bf(moe_w_up[0]), bf(moe_w_down[0]), ln_ffn_g[1], ln_ffn_b[1],
             p3d, 1, bf(ple_w_proj[1]), bf(ple_w_gate[1]), ple_b_gate[1])
    return h.reshape(batch, seq, d)
```

```python
import functools
import math

import jax
import jax.numpy as jnp
from jax import lax
from jax.experimental import pallas as pl
from jax.experimental.pallas import tpu as pltpu

F32 = jnp.float32
BF16 = jnp.bfloat16

D_MODEL = 1024
N_HEADS = 16
HEAD_DIM = D_MODEL // N_HEADS
CONV_WIDTH = 31
N_EXPERTS = 8
LN_EPS = 1e-5
DEPTH = 2
ALPHA = (2.0 * DEPTH) ** 0.25
LOG2E = math.log2(math.e)

LANES = 128
SUBLANES = 8
VMEM_LIMIT = 56 * 1024 * 1024

CONV_TM = 512
CONV_HALO = 32
FFN_TM = 512
QKV_TM = 512
ATT_T = 256
OUT_TM = 512
MOE_TILE = 512
DISPATCH_TB = 512
COMBINE_TM = 512


def _ln(x, g, b):
    mu = jnp.mean(x, axis=-1, keepdims=True)
    xc = x - mu
    var = jnp.mean(xc * xc, axis=-1, keepdims=True)
    return xc * lax.rsqrt(var + LN_EPS) * g + b


def _sigmoid(x):
    return 1.0 / (1.0 + jnp.exp(-x))


def _dot(a, b):
    return jnp.dot(a, b, preferred_element_type=F32)


def _const_spec(shape):
    nd = len(shape)
    return pl.BlockSpec(shape, lambda *_: (0,) * nd, pipeline_mode=pl.Buffered(1))


def _params(n_axes):
    return pltpu.CompilerParams(dimension_semantics=("arbitrary",) * n_axes,
                                vmem_limit_bytes=VMEM_LIMIT)


def _dwconv(g_scr, wb_ref, bdw_ref, c_scr, tm):
    d = g_scr.shape[1]
    off = CONV_HALO - (CONV_WIDTH - 1)
    sub = lax.broadcasted_iota(jnp.int32, (SUBLANES, LANES), 0)
    masks = [sub >= r for r in range(SUBLANES)]
    n_a = (off + CONV_WIDTH - 1) // SUBLANES

    def valid(m):
        return off <= m < off + CONV_WIDTH

    for l in range(d // LANES):
        cols = slice(l * LANES, (l + 1) * LANES)

        def wv(m, cols=cols):
            return wb_ref[m - off, :, cols]

        def tile(j, cols=cols):
            if isinstance(j, int):
                return g_scr[j * SUBLANES:(j + 1) * SUBLANES, cols]
            return g_scr[pl.ds(pl.multiple_of(j * SUBLANES, SUBLANES), SUBLANES), cols]

        def shifted_sums(t, wv=wv):
            out = []
            for r in range(1, SUBLANES):
                acc = None
                for a in range(n_a):
                    m = SUBLANES * a + r
                    if valid(m):
                        term = wv(m) * t[a]
                        acc = term if acc is None else acc + term
                out.append(acc)
            return tuple(out)

        bias = jnp.broadcast_to(bdw_ref[:, cols], (SUBLANES, LANES))

        def body(i, carry, cols=cols, bias=bias, wv=wv, tile=tile, shifted_sums=shifted_sums):
            t = [tile(i + 1 + a) for a in range(n_a)]
            nxt = shifted_sums(t)
            acc = bias
            for a in range(1, n_a + 1):
                if valid(SUBLANES * a):
                    acc = acc + wv(SUBLANES * a) * t[a - 1]
            for r in range(1, SUBLANES):
                y = jnp.where(masks[r], carry[r - 1], nxt[r - 1])
                acc = acc + pltpu.roll(y, SUBLANES - r, 0)
            c_scr[pl.ds(pl.multiple_of(i * SUBLANES, SUBLANES), SUBLANES), cols] = acc
            return nxt

        lax.fori_loop(0, tm // SUBLANES, body, shifted_sums([tile(a) for a in range(n_a)]), unroll=8)


def _conv_mixer_kernel(x_ref, w1_ref, b1_ref, wb_ref, bdw_ref, lng_ref, lnb_ref, w2_ref, b2_ref,
                       mg_ref, mb_ref, o_ref, g_scr, c_scr):
    tm = x_ref.shape[0]
    d = x_ref.shape[1]

    @pl.when(pl.program_id(1) == 0)
    def _():
        g_scr[0:CONV_HALO, :] = jnp.zeros((CONV_HALO, d), F32)

    x = x_ref[...]
    h = _dot(x.astype(BF16), w1_ref[...]) + b1_ref[...]
    g_scr[CONV_HALO:CONV_HALO + tm, :] = h[:, :d] * _sigmoid(h[:, d:])
    _dwconv(g_scr, wb_ref, bdw_ref, c_scr, tm)
    g_scr[0:CONV_HALO, :] = g_scr[tm:tm + CONV_HALO, :]

    hc = _ln(c_scr[...], lng_ref[...], lnb_ref[...])
    hc = hc * _sigmoid(hc)
    mix = _dot(hc.astype(BF16), w2_ref[...]) + b2_ref[...]
    o_ref[...] = _ln(ALPHA * x + mix, mg_ref[...], mb_ref[...])


def _conv_mixer(x2d, batch, seq, w1, b1, wdw, bdw, lng, lnb, w2, b2, mg, mb):
    t, d = x2d.shape
    tm = CONV_TM
    ns = seq // tm
    row = lambda v: v.reshape(1, -1)
    return pl.pallas_call(
        _conv_mixer_kernel,
        grid=(batch, ns),
        in_specs=[
            pl.BlockSpec((tm, d), lambda b, s: (b * ns + s, 0)),
            _const_spec((d, 2 * d)), _const_spec((1, 2 * d)),
            _const_spec((CONV_WIDTH, SUBLANES, d)), _const_spec((1, d)),
            _const_spec((1, d)), _const_spec((1, d)),
            _const_spec((d, d)), _const_spec((1, d)),
            _const_spec((1, d)), _const_spec((1, d)),
        ],
        out_specs=pl.BlockSpec((tm, d), lambda b, s: (b * ns + s, 0)),
        out_shape=jax.ShapeDtypeStruct((t, d), F32),
        scratch_shapes=[pltpu.VMEM((CONV_HALO + tm, d), F32), pltpu.VMEM((tm, d), F32)],
        compiler_params=_params(2),
        name="conv_mixer",
    )(x2d, w1, row(b1), jnp.broadcast_to(wdw[:, None, :], (CONV_WIDTH, SUBLANES, d)), row(bdw),
      row(lng), row(lnb), w2, row(b2), row(mg), row(mb))


def _ffn_epilogue(h, ff, lg_ref, lb_ref, p_ref, wproj_ref, wpg_ref, bpg_ref):
    y = _ln(ALPHA * h + ff, lg_ref[...], lb_ref[...])
    gate = _sigmoid(_dot(y.astype(BF16), wpg_ref[...]) + bpg_ref[...])
    return y + gate * _dot(p_ref[...].astype(BF16), wproj_ref[...])


def _ffn0_kernel(h_ref, wg_ref, wu_ref, wd_ref, lg_ref, lb_ref, p_ref, wproj_ref, wpg_ref, bpg_ref, o_ref):
    h = h_ref[...]
    hb = h.astype(BF16)
    gate = _dot(hb, wg_ref[...])
    up = _dot(hb, wu_ref[...])
    act = (gate * _sigmoid(gate) * up).astype(BF16)
    ff = _dot(act, wd_ref[...])
    o_ref[...] = _ffn_epilogue(h, ff, lg_ref, lb_ref, p_ref, wproj_ref, wpg_ref, bpg_ref)


def _ffn0(h, wg, wu, wd, lg, lb, p3d, layer, wproj, wpg, bpg):
    t, d = h.shape
    f = wg.shape[1]
    dp = p3d.shape[2]
    tm = FFN_TM
    row = lambda v: v.reshape(1, -1)
    return pl.pallas_call(
        _ffn0_kernel,
        grid=(t // tm,),
        in_specs=[
            pl.BlockSpec((tm, d), lambda i: (i, 0)),
            _const_spec((d, f)), _const_spec((d, f)), _const_spec((f, d)),
            _const_spec((1, d)), _const_spec((1, d)),
            pl.BlockSpec((None, tm, dp), lambda i: (layer, i, 0)),
            _const_spec((dp, d)), _const_spec((d, d)), _const_spec((1, d)),
        ],
        out_specs=pl.BlockSpec((tm, d), lambda i: (i, 0)),
        out_shape=jax.ShapeDtypeStruct((t, d), F32),
        compiler_params=_params(1),
        name="ffn0",
    )(h, wg, wu, wd, row(lg), row(lb), p3d, wproj, wpg, row(bpg))


def _qkv_kernel(h_ref, w_ref, o_ref):
    d = h_ref.shape[1]
    qkv = _dot(h_ref[...].astype(BF16), w_ref[...])
    o_ref[:, :d] = (qkv[:, :d] * (HEAD_DIM ** -0.5 * LOG2E)).astype(BF16)
    o_ref[:, d:] = qkv[:, d:].astype(BF16)


def _qkv(h, w):
    t, d = h.shape
    n = w.shape[1]
    tm = QKV_TM
    return pl.pallas_call(
        _qkv_kernel,
        grid=(t // tm,),
        in_specs=[pl.BlockSpec((tm, d), lambda i: (i, 0)), _const_spec((d, n))],
        out_specs=pl.BlockSpec((tm, n), lambda i: (i, 0)),
        out_shape=jax.ShapeDtypeStruct((t, n), BF16),
        compiler_params=_params(1),
        name="qkv_proj",
    )(h, w)


def _attn_kernel(q_ref, k_ref, v_ref, tri_ref, o_ref, qs_scr, acc_scr, r_scr, *stage_scr):
    tq = ATT_T
    n_q = q_ref.shape[0] // tq
    stages = (stage_scr[:2], stage_scr[2:])
    items = [(qi, kb, h) for qi in range(n_q) for kb in range(qi, -1, -1) for h in (0, 1)]

    def rows(h):
        return slice(h * tq, (h + 1) * tq)

    def block(i):
        return slice(i * tq, (i + 1) * tq)

    lane = lax.broadcasted_iota(jnp.int32, (tq, LANES), 1)
    for qi in range(n_q):
        nq = -q_ref[block(qi), :]
        zero = jnp.zeros_like(nq)
        qs_scr[qi, rows(0), :] = jnp.where(lane < HEAD_DIM, nq, zero)
        qs_scr[qi, rows(1), :] = jnp.where(lane >= HEAD_DIM, nq, zero)

    def score_matmul(n):
        qi, kb, h = items[n]
        return lax.dot_general(qs_scr[qi, rows(h), :], k_ref[block(kb), :], (((1,), (1,)), ((), ())),
                               preferred_element_type=F32)

    def score_rest(zn, n):
        qi, kb, h = items[n]
        zn_scr, l_scr = stages[(n // 2) % 2]
        if kb == qi:
            row = lax.broadcasted_iota(jnp.int32, zn.shape, 0)
            col = lax.broadcasted_iota(jnp.int32, zn.shape, 1)
            zn = jnp.where(col < row, zn, jnp.inf)
        log1m = jnp.minimum(zn, 0.0) - jnp.log(1.0 + jnp.exp2(-jnp.abs(zn))) * LOG2E
        zn_scr[rows(h), :] = zn
        l_scr[rows(h), :] = log1m.astype(BF16)

    def cumsum_matmul(n):
        h = items[n][2]
        return _dot(stages[(n // 2) % 2][1][rows(h), :], tri_ref[...])

    def accumulate_rest(csum, n):
        qi, kb, h = items[n]
        zn = stages[(n // 2) % 2][0][rows(h), :]
        if kb == qi:
            w = jnp.exp2(csum - zn)
            carried = csum[:, 0:1]
        else:
            r = r_scr[rows(h), :]
            w = jnp.exp2(csum + r - zn)
            carried = r + csum[:, 0:1]
        acc = _dot(w.astype(BF16), v_ref[block(kb), :])
        if kb != qi:
            acc = acc_scr[rows(h), :] + acc
        if kb != 0:
            r_scr[rows(h), :] = carried
            acc_scr[rows(h), :] = acc
        elif h == 0:
            acc_scr[rows(h), :] = acc
        else:
            o_ref[block(qi), :] = jnp.where(lane < HEAD_DIM, acc_scr[rows(0), :], acc).astype(BF16)

    n_items = len(items)
    score_rest(score_matmul(0), 0)
    score_rest(score_matmul(1), 1)
    csums, scores = {}, {}
    for n in range(n_items):
        csums[n] = cumsum_matmul(n)
        if n + 2 < n_items:
            scores[n + 2] = score_matmul(n + 2)
        if n >= 1:
            accumulate_rest(csums.pop(n - 1), n - 1)
            if n + 1 < n_items:
                score_rest(scores.pop(n + 1), n + 1)
    accumulate_rest(csums.pop(n_items - 1), n_items - 1)


def _attention(qkv, batch, seq):
    t = qkv.shape[0]
    tq = ATT_T
    nq = seq // tq
    hp = D_MODEL // LANES
    j = lax.broadcasted_iota(jnp.int32, (tq, tq), 0)
    s = lax.broadcasted_iota(jnp.int32, (tq, tq), 1)
    tri = (j >= s).astype(BF16)
    return pl.pallas_call(
        _attn_kernel,
        grid=(batch, hp),
        in_specs=[
            pl.BlockSpec((seq, LANES), lambda b, h: (b, h)),
            pl.BlockSpec((seq, LANES), lambda b, h: (b, hp + h)),
            pl.BlockSpec((seq, LANES), lambda b, h: (b, 2 * hp + h)),
            _const_spec((tq, tq)),
        ],
        out_specs=pl.BlockSpec((seq, LANES), lambda b, h: (b, h)),
        out_shape=jax.ShapeDtypeStruct((t, D_MODEL), BF16),
        scratch_shapes=[pltpu.VMEM((nq, 2 * tq, LANES), BF16), pltpu.VMEM((2 * tq, LANES), F32),
                        pltpu.VMEM((2 * tq, 1), F32)]
        + 2 * [pltpu.VMEM((2 * tq, tq), F32), pltpu.VMEM((2 * tq, tq), BF16)],
        compiler_params=_params(2),
        name="stick_attn",
    )(qkv, qkv, qkv, tri)


ROUTE_E1, ROUTE_E2, ROUTE_W1, ROUTE_W2, ROUTE_R1, ROUTE_R2 = range(6)


def _attn_out_kernel(o_ref, h_ref, wo_ref, mg_ref, mb_ref, wrh_ref, wrl_ref, br_ref, ltri_ref,
                     y_ref, route_ref, cnt_ref, cnt_scr):
    @pl.when(pl.program_id(0) == 0)
    def _():
        cnt_scr[...] = jnp.zeros(cnt_scr.shape, F32)

    y = _ln(ALPHA * h_ref[...] + _dot(o_ref[...], wo_ref[...]), mg_ref[...], mb_ref[...])
    y_ref[...] = y
    yh = y.astype(BF16)
    yl = (y - yh.astype(F32)).astype(BF16)
    logits = _dot(yh, wrh_ref[...]) + _dot(yl, wrh_ref[...]) + _dot(yh, wrl_ref[...]) + br_ref[...]
    lane = lax.broadcasted_iota(jnp.int32, logits.shape, 1)
    m1 = jnp.max(logits, axis=-1, keepdims=True)
    i1 = jnp.min(jnp.where(logits == m1, lane, LANES), axis=-1, keepdims=True)
    rest = jnp.where(lane == i1, -jnp.inf, logits)
    m2 = jnp.max(rest, axis=-1, keepdims=True)
    i2 = jnp.min(jnp.where(rest == m2, lane, LANES), axis=-1, keepdims=True)
    e = jnp.exp(m2 - m1)
    w1 = 1.0 / (1.0 + e)
    w2 = e / (1.0 + e)
    oh1 = lane == i1
    oh2 = lane == i2
    sel = jnp.where(oh1 | oh2, 1.0, 0.0)
    before = _dot(ltri_ref[...], sel.astype(BF16)) + cnt_scr[...]
    rank1 = jnp.sum(jnp.where(oh1, before, 0.0), axis=-1, keepdims=True)
    rank2 = jnp.sum(jnp.where(oh2, before, 0.0), axis=-1, keepdims=True)
    cnt = cnt_scr[...] + jnp.sum(sel, axis=0, keepdims=True)
    cnt_scr[...] = cnt
    cnt_ref[...] = cnt
    route = jnp.zeros(logits.shape, F32)
    for c, v in ((ROUTE_E1, i1.astype(F32)), (ROUTE_E2, i2.astype(F32)), (ROUTE_W1, w1), (ROUTE_W2, w2),
                 (ROUTE_R1, rank1), (ROUTE_R2, rank2)):
        route = jnp.where(lane == c, v, route)
    route_ref[...] = route


def _attn_out(o, h, wo, mg, mb, wr_hi, wr_lo, br):
    t, d = h.shape
    tm = OUT_TM
    row = lambda v: v.reshape(1, -1)
    r = lax.broadcasted_iota(jnp.int32, (tm, tm), 0)
    c = lax.broadcasted_iota(jnp.int32, (tm, tm), 1)
    ltri = (c < r).astype(BF16)
    return pl.pallas_call(
        _attn_out_kernel,
        grid=(t // tm,),
        in_specs=[
            pl.BlockSpec((tm, d), lambda i: (i, 0)),
            pl.BlockSpec((tm, d), lambda i: (i, 0)),
            _const_spec((d, d)), _const_spec((1, d)), _const_spec((1, d)),
            _const_spec((d, LANES)), _const_spec((d, LANES)), _const_spec((1, LANES)),
            _const_spec((tm, tm)),
        ],
        out_specs=[pl.BlockSpec((tm, d), lambda i: (i, 0)), pl.BlockSpec((tm, LANES), lambda i: (i, 0)),
                   pl.BlockSpec((1, LANES), lambda i: (0, 0))],
        out_shape=[jax.ShapeDtypeStruct((t, d), F32), jax.ShapeDtypeStruct((t, LANES), F32),
                   jax.ShapeDtypeStruct((1, LANES), F32)],
        scratch_shapes=[pltpu.VMEM((1, LANES), F32)],
        compiler_params=_params(1),
        name="attn_out_router",
    )(o, h, wo, row(mg), row(mb), wr_hi, wr_lo, br, ltri)


def _route_plan(route, cnt, tile):
    t = route.shape[0]
    as_int = lambda lane: route[:, lane].astype(jnp.int32)
    counts = cnt[0, :N_EXPERTS].astype(jnp.int32)
    ends = jnp.cumsum(counts)
    starts = ends - counts
    dest1 = jnp.take(starts, as_int(ROUTE_E1)) + as_int(ROUTE_R1)
    dest2 = jnp.take(starts, as_int(ROUTE_E2)) + as_int(ROUTE_R2)

    n_tiles = 2 * t // tile
    n_items = n_tiles + N_EXPERTS - 1
    tile_lo = jnp.arange(n_tiles, dtype=jnp.int32)[:, None] * tile
    lo = jnp.maximum(tile_lo, starts[None, :])
    hi = jnp.minimum(tile_lo + tile, ends[None, :])
    valid = (hi > lo).reshape(-1)
    n_valid = jnp.sum(valid.astype(jnp.int32))
    (idx,) = jnp.nonzero(valid, size=n_items, fill_value=0)
    idx = idx.astype(jnp.int32)
    is_pad = jnp.arange(n_items, dtype=jnp.int32) >= n_valid
    idx = jnp.where(is_pad, idx[n_valid - 1], idx)
    it_tile = idx // N_EXPERTS
    it_exp = idx % N_EXPERTS
    it_lo = jnp.where(is_pad, 0, lo.reshape(-1)[idx] - it_tile * tile)
    it_hi = jnp.where(is_pad, 0, hi.reshape(-1)[idx] - it_tile * tile)
    it_first = jnp.concatenate([jnp.ones((1,), jnp.int32), (it_tile[1:] != it_tile[:-1]).astype(jnp.int32)])
    return dest1, dest2, (it_tile, it_exp, it_lo, it_hi, it_first)


def _row_copy(src, src_row, dst, dst_row, sem):
    return pltpu.make_async_copy(src.at[pl.ds(src_row, 1), :], dst.at[pl.ds(dst_row, 1), :], sem)


def _dispatch_kernel(d1_ref, d2_ref, h_ref, xs_hbm, sem):
    base = pl.program_id(0) * DISPATCH_TB

    def copies(j):
        t = base + j
        return (_row_copy(h_ref, j, xs_hbm, d1_ref[t], sem), _row_copy(h_ref, j, xs_hbm, d2_ref[t], sem))

    def start(j, carry):
        for cp in copies(j):
            cp.start()
        return carry

    def wait(j, carry):
        for cp in copies(j):
            cp.wait()
        return carry

    lax.fori_loop(0, DISPATCH_TB, start, 0, unroll=8)
    lax.fori_loop(0, DISPATCH_TB, wait, 0, unroll=8)


def _dispatch(h, dest1, dest2):
    t, d = h.shape
    return pl.pallas_call(
        _dispatch_kernel,
        grid_spec=pltpu.PrefetchScalarGridSpec(
            num_scalar_prefetch=2,
            grid=(t // DISPATCH_TB,),
            in_specs=[pl.BlockSpec((DISPATCH_TB, d), lambda i, d1, d2: (i, 0))],
            out_specs=pl.BlockSpec(memory_space=pl.ANY),
            scratch_shapes=[pltpu.SemaphoreType.DMA(())],
        ),
        out_shape=jax.ShapeDtypeStruct((2 * t, d), F32),
        compiler_params=_params(1),
        name="moe_dispatch",
    )(dest1, dest2, h)


def _grouped_kernel(tile_ref, exp_ref, lo_ref, hi_ref, first_ref, x_ref, wg_ref, wu_ref, wd_ref, o_ref):
    i = pl.program_id(0)
    lo = lo_ref[i]
    hi = hi_ref[i]

    @pl.when(hi > lo)
    def _():
        xb = x_ref[...].astype(BF16)
        gate = _dot(xb, wg_ref[...])
        up = _dot(xb, wu_ref[...])
        act = (gate * _sigmoid(gate) * up).astype(BF16)
        y = _dot(act, wd_ref[...])
        row = lax.broadcasted_iota(jnp.int32, y.shape, 0)
        mine = (row >= lo) & (row < hi)

        @pl.when(first_ref[i] == 1)
        def _():
            o_ref[...] = jnp.where(mine, y, 0.0)

        @pl.when(first_ref[i] == 0)
        def _():
            o_ref[...] = jnp.where(mine, y, o_ref[...])


def _grouped_experts(xs, items, wg, wu, wd):
    r, d = xs.shape
    ne, _, f = wg.shape
    tile = MOE_TILE
    n_items = items[0].shape[0]
    return pl.pallas_call(
        _grouped_kernel,
        grid_spec=pltpu.PrefetchScalarGridSpec(
            num_scalar_prefetch=5,
            grid=(n_items,),
            in_specs=[
                pl.BlockSpec((tile, d), lambda i, tl, ex, lo, hi, fi: (tl[i], 0)),
                pl.BlockSpec((None, d, f), lambda i, tl, ex, lo, hi, fi: (ex[i], 0, 0)),
                pl.BlockSpec((None, d, f), lambda i, tl, ex, lo, hi, fi: (ex[i], 0, 0)),
                pl.BlockSpec((None, f, d), lambda i, tl, ex, lo, hi, fi: (ex[i], 0, 0)),
            ],
            out_specs=pl.BlockSpec((tile, d), lambda i, tl, ex, lo, hi, fi: (tl[i], 0)),
        ),
        out_shape=jax.ShapeDtypeStruct((r, d), F32),
        compiler_params=_params(1),
        name="moe_grouped",
    )(*items, xs, wg, wu, wd)


def _combine_kernel(d1_ref, d2_ref, ys_hbm, h_ref, route_ref, lg_ref, lb_ref, p_ref, wproj_ref, wpg_ref, bpg_ref,
                    o_ref, y_scr, sem):
    tm = h_ref.shape[0]
    i = pl.program_id(0)
    last = pl.num_programs(0) - 1
    slot = i % 2

    def copies(tile, s, j):
        t = tile * tm + j
        return (_row_copy(ys_hbm, d1_ref[t], y_scr.at[s, 0], j, sem.at[s]),
                _row_copy(ys_hbm, d2_ref[t], y_scr.at[s, 1], j, sem.at[s]))

    def start_loop(tile, s):
        def body(j, carry):
            for cp in copies(tile, s, j):
                cp.start()
            return carry
        lax.fori_loop(0, tm, body, 0, unroll=8)

    def wait_loop(tile, s):
        def body(j, carry):
            for cp in copies(tile, s, j):
                cp.wait()
            return carry
        lax.fori_loop(0, tm, body, 0, unroll=8)

    @pl.when(i == 0)
    def _():
        start_loop(0, 0)

    wait_loop(i, slot)

    route = route_ref[...]
    lane = lax.broadcasted_iota(jnp.int32, route.shape, 1)
    w1 = jnp.sum(jnp.where(lane == ROUTE_W1, route, 0.0), axis=-1, keepdims=True)
    w2 = jnp.sum(jnp.where(lane == ROUTE_W2, route, 0.0), axis=-1, keepdims=True)
    ff = w1 * y_scr[slot, 0] + w2 * y_scr[slot, 1]
    o_ref[...] = _ffn_epilogue(h_ref[...], ff, lg_ref, lb_ref, p_ref, wproj_ref, wpg_ref, bpg_ref)

    nxt = jnp.minimum(i + 1, last)
    for j in range(tm):
        for cp in copies(nxt, 1 - slot, j):
            cp.start()

    @pl.when(i == last)
    def _():
        wait_loop(last, 1 - slot)


def _combine(ys, dest1, dest2, h, route, lg, lb, p3d, layer, wproj, wpg, bpg):
    t, d = h.shape
    dp = p3d.shape[2]
    tm = COMBINE_TM
    row = lambda v: v.reshape(1, -1)
    const = lambda shape: pl.BlockSpec(shape, lambda i, d1, d2: (0,) * len(shape), pipeline_mode=pl.Buffered(1))
    return pl.pallas_call(
        _combine_kernel,
        grid_spec=pltpu.PrefetchScalarGridSpec(
            num_scalar_prefetch=2,
            grid=(t // tm,),
            in_specs=[
                pl.BlockSpec(memory_space=pl.ANY),
                pl.BlockSpec((tm, d), lambda i, d1, d2: (i, 0)),
                pl.BlockSpec((tm, LANES), lambda i, d1, d2: (i, 0)),
                const((1, d)), const((1, d)),
                pl.BlockSpec((None, tm, dp), lambda i, d1, d2: (layer, i, 0)),
                const((dp, d)), const((d, d)), const((1, d)),
            ],
            out_specs=pl.BlockSpec((tm, d), lambda i, d1, d2: (i, 0)),
            scratch_shapes=[pltpu.VMEM((2, 2, tm, d), F32), pltpu.SemaphoreType.DMA((2,))],
        ),
        out_shape=jax.ShapeDtypeStruct((t, d), F32),
        compiler_params=_params(1),
        name="moe_combine",
    )(dest1, dest2, ys, h, route, row(lg), row(lb), p3d, wproj, wpg, row(bpg))


def _moe(h, route, cnt, wg, wu, wd, lg, lb, p3d, layer, wproj, wpg, bpg):
    dest1, dest2, items = _route_plan(route, cnt, MOE_TILE)
    xs = _dispatch(h, dest1, dest2)
    ys = _grouped_experts(xs, items, wg, wu, wd)
    return _combine(ys, dest1, dest2, h, route, lg, lb, p3d, layer, wproj, wpg, bpg)


def kernel(x, p, conv_w_pw1, conv_b_pw1, conv_w_dw, conv_b_dw, conv_ln_g, conv_ln_b, conv_w_pw2, conv_b_pw2, attn_w_qkv, attn_w_o, ffn_w_gate, ffn_w_up, ffn_w_down, moe_w_router, moe_b_router, moe_w_gate, moe_w_up, moe_w_down, ln_mix_g, ln_mix_b, ln_ffn_g, ln_ffn_b, ple_w_proj, ple_w_gate, ple_b_gate):
    batch, seq, d = x.shape
    t = batch * seq
    bf = lambda w: w.astype(BF16)
    x2d = x.reshape(t, d)
    p3d = p.reshape(p.shape[0], t, p.shape[-1])

    h = _conv_mixer(x2d, batch, seq, bf(conv_w_pw1[0]), conv_b_pw1[0], conv_w_dw[0], conv_b_dw[0],
                    conv_ln_g[0], conv_ln_b[0], bf(conv_w_pw2[0]), conv_b_pw2[0], ln_mix_g[0], ln_mix_b[0])
    h = _ffn0(h, bf(ffn_w_gate[0]), bf(ffn_w_up[0]), bf(ffn_w_down[0]), ln_ffn_g[0], ln_ffn_b[0],
              p3d, 0, bf(ple_w_proj[0]), bf(ple_w_gate[0]), ple_b_gate[0])

    qkv = _qkv(h, bf(attn_w_qkv[0]))
    o = _attention(qkv, batch, seq)
    wr = jnp.zeros((d, LANES), F32).at[:, :N_EXPERTS].set(moe_w_router[0])
    wr_hi = wr.astype(BF16)
    wr_lo = (wr - wr_hi.astype(F32)).astype(BF16)
    br = jnp.full((1, LANES), -1e30, F32).at[0, :N_EXPERTS].set(moe_b_router[0])
    h, route, cnt = _attn_out(o, h, bf(attn_w_o[0]), ln_mix_g[1], ln_mix_b[1], wr_hi, wr_lo, br)
    h = _moe(h, route, cnt, bf(moe_w_gate[0]), bf(moe_w_up[0]), bf(moe_w_down[0]), ln_ffn_g[1], ln_ffn_b[1],
             p3d, 1, bf(ple_w_proj[1]), bf(ple_w_gate[1]), ple_b_gate[1])
    return h.reshape(batch, seq, d)
```
